```python
import math
import jax, jax.numpy as jnp
from jax import lax
import numpy as np

D_MODEL = 1024
BATCH = 2
SEQ = 8192
DEPTH = 2
DEC_BATCH = 128
DEC_SEQ = 8
PAST_LEN = 2048
PAGE_SIZE = 128

N_HEADS = 8
HEAD_DIM = 64
ATTN_WIDTH = N_HEADS * HEAD_DIM
MOBA_BLOCK = 256
MOBA_TOPK = 3
Q_CHUNK = 64
POOL_WINDOWS = (2, 4, 8, 16)
POOL_GROUPS = len(POOL_WINDOWS)
POOL_GROUP_WIDTH = 128
POOL_WIDTH = POOL_GROUPS * POOL_GROUP_WIDTH
POOL_BUF = max(POOL_WINDOWS) - 1
ROPE_THETA = 10000.0
RMS_EPS = 1e-6
NEG_INF = -1e30
_SIZES = (ATTN_WIDTH, ATTN_WIDTH, ATTN_WIDTH, ATTN_WIDTH, POOL_WIDTH, POOL_WIDTH, D_MODEL, D_MODEL)
IN_WIDTH = sum(_SIZES)
SPLIT_POINTS = tuple(sum(_SIZES[:i + 1]) for i in range(len(_SIZES) - 1))

kernel_name = 'hybrid_moba_pool_gated_decoder_step'


def rms_norm(x, w):
    xf = x.astype(jnp.float32)
    y = xf * lax.rsqrt(jnp.mean(xf * xf, axis=-1, keepdims=True) + RMS_EPS)
    return (y * w.astype(jnp.float32)).astype(x.dtype)


def rope(x, pos):
    half = HEAD_DIM // 2
    inv_freq = jnp.exp(-math.log(ROPE_THETA) * jnp.arange(half, dtype=jnp.float32) / half)
    ang = pos.astype(jnp.float32)[:, None] * inv_freq[None, :]
    cos = jnp.cos(ang)[None, :, None, :]
    sin = jnp.sin(ang)[None, :, None, :]
    xf = x.astype(jnp.float32)
    x1, x2 = xf[..., :half], xf[..., half:]
    return jnp.concatenate([x1 * cos - x2 * sin, x1 * sin + x2 * cos], axis=-1).astype(x.dtype)


def project_inputs(x, pos, norm_w, w_in, q_norm_w, k_norm_w):
    n, t = x.shape[0], x.shape[1]
    h = rms_norm(x, norm_w)
    z = jnp.einsum('ntd,de->nte', h, w_in)
    q, k, v, gate_a, u, gate_p, g_a, g_b = jnp.split(z, SPLIT_POINTS, axis=-1)
    q = rope(rms_norm(q.reshape(n, t, N_HEADS, HEAD_DIM), q_norm_w), pos)
    k = rope(rms_norm(k.reshape(n, t, N_HEADS, HEAD_DIM), k_norm_w), pos)
    v = v.reshape(n, t, N_HEADS, HEAD_DIM)
    return q, k, v, gate_a, u, gate_p, g_a, g_b


def merge_branches(x, attn, gate_a, pool, gate_p, g_a, g_b, w_pa, w_pb, w_o):
    a = jnp.einsum('nte,ed->ntd', attn * jax.nn.silu(gate_a), w_pa)
    b = jnp.einsum('nte,ed->ntd', pool * jax.nn.silu(gate_p), w_pb)
    y = jax.nn.sigmoid(g_a) * a + jax.nn.sigmoid(g_b) * b
    return x + jnp.einsum('ntd,de->nte', y, w_o)


def pool_mix(u, buf, start_pos, w_grp, scale):
    n, t, c = u.shape
    full = jnp.concatenate([buf, u], axis=1)
    cs = jnp.cumsum(full.astype(jnp.float32), axis=1)
    cs = jnp.concatenate([jnp.zeros((n, 1, c), jnp.float32), cs], axis=1)
    upper = cs[:, POOL_BUF + 1:POOL_BUF + 1 + t]
    pos = start_pos + jnp.arange(t, dtype=jnp.int32)
    outs = []
    for g, w in enumerate(POOL_WINDOWS):
        sl = slice(g * POOL_GROUP_WIDTH, (g + 1) * POOL_GROUP_WIDTH)
        lower = cs[:, POOL_BUF + 1 - w:POOL_BUF + 1 - w + t, sl]
        cnt = jnp.minimum(w, pos + 1).astype(jnp.float32)[None, :, None]
        d = ((upper[..., sl] - lower) / cnt - u[..., sl].astype(jnp.float32)).astype(u.dtype)
        outs.append(jnp.einsum('ntc,ce->nte', d, w_grp[g]))
    out = jnp.concatenate(outs, axis=-1) * scale
    return out, full[:, -POOL_BUF:]


def moba_prompt(q, k, v):
    b, h, s, _ = q.shape
    nb = -(-s // MOBA_BLOCK)
    pad = nb * MOBA_BLOCK - s
    kblk = jnp.pad(k, ((0, 0), (0, 0), (0, pad), (0, 0))).reshape(b, h, nb, MOBA_BLOCK, HEAD_DIM)
    vblk = jnp.pad(v, ((0, 0), (0, 0), (0, pad), (0, 0))).reshape(b, h, nb, MOBA_BLOCK, HEAD_DIM)
    means = kblk.astype(jnp.float32).mean(axis=3).astype(k.dtype)
    n_sel = min(MOBA_TOPK, nb - 1)
    scale = HEAD_DIM ** -0.5
    bi = jnp.arange(b)[:, None, None, None]
    hi = jnp.arange(h)[None, :, None, None]
    blk_ids = jnp.arange(nb)

    def one_chunk(c):
        s0 = c * Q_CHUNK
        qc = lax.dynamic_slice_in_dim(q, s0, Q_CHUNK, axis=2)
        qpos = s0 + jnp.arange(Q_CHUNK)
        blk = s0 // MOBA_BLOCK
        ko = lax.dynamic_index_in_dim(kblk, blk, axis=2, keepdims=False)
        vo = lax.dynamic_index_in_dim(vblk, blk, axis=2, keepdims=False)
        own_mask = (blk * MOBA_BLOCK + jnp.arange(MOBA_BLOCK))[None, :] <= qpos[:, None]
        lo = jnp.einsum('bhqd,bhkd->bhqk', qc, ko).astype(jnp.float32) * scale
        lo = jnp.where(own_mask, lo, NEG_INF)
        if n_sel == 0:
            p = jax.nn.softmax(lo, axis=-1).astype(v.dtype)
            return jnp.einsum('bhqk,bhkd->bhqd', p, vo)
        sc = jnp.einsum('bhqd,bhnd->bhqn', qc, means).astype(jnp.float32)
        sc = jnp.where(blk_ids < blk, sc, -jnp.inf)
        _, idx = lax.top_k(sc, n_sel)
        valid = idx < blk
        ks = kblk[bi, hi, idx]
        vs = vblk[bi, hi, idx]
        ls = jnp.einsum('bhqd,bhqnkd->bhqnk', qc, ks).astype(jnp.float32) * scale
        ls = jnp.where(valid[..., None], ls, NEG_INF).reshape(b, h, Q_CHUNK, n_sel * MOBA_BLOCK)
        p = jax.nn.softmax(jnp.concatenate([lo, ls], axis=-1), axis=-1).astype(v.dtype)
        out = jnp.einsum('bhqk,bhkd->bhqd', p[..., :MOBA_BLOCK], vo)
        ps = p[..., MOBA_BLOCK:].reshape(b, h, Q_CHUNK, n_sel, MOBA_BLOCK)
        return out + jnp.einsum('bhqnk,bhqnkd->bhqd', ps, vs)

    out = lax.map(one_chunk, jnp.arange(s // Q_CHUNK))
    return out.transpose(1, 2, 0, 3, 4).reshape(b, h, s, HEAD_DIM)


def moba_sample(q, k, v, ck, cv, page_table):
    h, ds = q.shape[1], q.shape[2]
    n_pages = page_table.shape[1]
    past = n_pages * PAGE_SIZE
    ppb = MOBA_BLOCK // PAGE_SIZE
    n_full = past // MOBA_BLOCK
    r0 = n_full * MOBA_BLOCK
    n_own_past = past - r0
    n_sel = min(MOBA_TOPK, n_full)
    scale = HEAD_DIM ** -0.5
    hi = jnp.arange(h)[:, None, None]
    own_mask = jnp.concatenate([jnp.ones((ds, n_own_past), bool), jnp.tril(jnp.ones((ds, ds), bool))], axis=1)

    def one_seq(args):
        qn, kn, vn, prow = args
        own_pages = prow[n_full * ppb:]
        ko = jnp.concatenate([ck[own_pages].reshape(n_own_past, h, HEAD_DIM).transpose(1, 0, 2), kn], axis=1)
        vo = jnp.concatenate([cv[own_pages].reshape(n_own_past, h, HEAD_DIM).transpose(1, 0, 2), vn], axis=1)
        lo = jnp.einsum('hqd,hkd->hqk', qn, ko).astype(jnp.float32) * scale
        lo = jnp.where(own_mask, lo, NEG_INF)
        if n_sel == 0:
            p = jax.nn.softmax(lo, axis=-1).astype(vn.dtype)
            return jnp.einsum('hqk,hkd->hqd', p, vo)
        kpast = ck[prow[:n_full * ppb]].reshape(n_full, MOBA_BLOCK, h, HEAD_DIM)
        means = kpast.astype(jnp.float32).mean(axis=1).astype(kpast.dtype)
        sc = jnp.einsum('hqd,nhd->hqn', qn, means).astype(jnp.float32)
        _, idx = lax.top_k(sc, n_sel)
        ks = kpast[idx, :, hi, :].reshape(h, ds, n_sel * MOBA_BLOCK, HEAD_DIM)
        phys = prow[idx[..., None] * ppb + jnp.arange(ppb)]
        vs = cv[phys, :, hi[..., None], :].reshape(h, ds, n_sel * MOBA_BLOCK, HEAD_DIM)
        ls = jnp.einsum('hqd,hqkd->hqk', qn, ks).astype(jnp.float32) * scale
        p = jax.nn.softmax(jnp.concatenate([lo, ls], axis=-1), axis=-1).astype(vn.dtype)
        n_own = n_own_past + ds
        return jnp.einsum('hqk,hkd->hqd', p[..., :n_own], vo) + jnp.einsum('hqk,hqkd->hqd', p[..., n_own:], vs)

    return lax.map(one_seq, (q, k, v, page_table))


def setup_inputs(seed: int = 0) -> dict:
    key = jax.random.key(seed)
    ks = jax.random.split(key, 16)
    n_pages = PAST_LEN // PAGE_SIZE
    n_used = DEC_BATCH * n_pages
    n_phys = (5 * n_used + 3) // 4
    nrm = jax.random.normal
    x_prompt = nrm(ks[0], (BATCH, SEQ, D_MODEL), jnp.float32)
    x_sample = nrm(ks[1], (DEC_BATCH, DEC_SEQ, D_MODEL), jnp.float32)
    cache_k = nrm(ks[2], (DEPTH, n_phys, PAGE_SIZE, N_HEADS, HEAD_DIM), jnp.float32)
    cache_v = nrm(ks[3], (DEPTH, n_phys, PAGE_SIZE, N_HEADS, HEAD_DIM), jnp.float32)
    state_pool = nrm(ks[4], (DEPTH, DEC_BATCH, POOL_BUF, POOL_WIDTH), jnp.float32)
    page_table = jax.random.permutation(ks[5], n_phys)[:n_used].reshape(DEC_BATCH, n_pages).astype(jnp.int32)
    norm_w = 1.0 + 0.05 * nrm(ks[6], (DEPTH, D_MODEL), jnp.float32)
    w_in = nrm(ks[7], (DEPTH, D_MODEL, IN_WIDTH), jnp.float32) * D_MODEL ** -0.5
    q_norm_w = 1.0 + 0.05 * nrm(ks[8], (DEPTH, HEAD_DIM), jnp.float32)
    k_norm_w = 1.0 + 0.05 * nrm(ks[9], (DEPTH, HEAD_DIM), jnp.float32)
    w_pool_grp = nrm(ks[10], (DEPTH, POOL_GROUPS, POOL_GROUP_WIDTH, POOL_GROUP_WIDTH), jnp.float32) * POOL_GROUP_WIDTH ** -0.5
    pool_scale = 1.0 + 0.05 * nrm(ks[11], (DEPTH, POOL_WIDTH), jnp.float32)
    w_proj_attn = nrm(ks[12], (DEPTH, ATTN_WIDTH, D_MODEL), jnp.float32) * ATTN_WIDTH ** -0.5
    w_proj_pool = nrm(ks[13], (DEPTH, POOL_WIDTH, D_MODEL), jnp.float32) * POOL_WIDTH ** -0.5
    w_out = nrm(ks[14], (DEPTH, D_MODEL, D_MODEL), jnp.float32) * D_MODEL ** -0.5
    return {'x_prompt': x_prompt, 'x_sample': x_sample, 'cache_k': cache_k, 'cache_v': cache_v,
            'state_pool': state_pool, 'page_table': page_table, 'norm_w': norm_w, 'w_in': w_in,
            'q_norm_w': q_norm_w, 'k_norm_w': k_norm_w, 'w_pool_grp': w_pool_grp, 'pool_scale': pool_scale,
            'w_proj_attn': w_proj_attn, 'w_proj_pool': w_proj_pool, 'w_out': w_out}


def reference(x_prompt, x_sample, cache_k, cache_v, state_pool, page_table, norm_w, w_in,
              q_norm_w, k_norm_w, w_pool_grp, pool_scale, w_proj_attn, w_proj_pool, w_out):
    bp, sp = x_prompt.shape[0], x_prompt.shape[1]
    bs, ss = x_sample.shape[0], x_sample.shape[1]
    pos_p = jnp.arange(sp, dtype=jnp.int32)
    pos_s = PAST_LEN + jnp.arange(ss, dtype=jnp.int32)
    xp, xs = x_prompt, x_sample
    kp_l, vp_l, pp_l, ks_l, vs_l, ps_l = [], [], [], [], [], []
    for l in range(DEPTH):
        q, k, v, ga, u, gp, g_a, g_b = project_inputs(xp, pos_p, norm_w[l], w_in[l], q_norm_w[l], k_norm_w[l])
        attn = moba_prompt(q.transpose(0, 2, 1, 3), k.transpose(0, 2, 1, 3), v.transpose(0, 2, 1, 3))
        attn = attn.transpose(0, 2, 1, 3).reshape(bp, sp, ATTN_WIDTH)
        pool, pbuf = pool_mix(u, jnp.zeros((bp, POOL_BUF, POOL_WIDTH), u.dtype), 0, w_pool_grp[l], pool_scale[l])
        xp = merge_branches(xp, attn, ga, pool, gp, g_a, g_b, w_proj_attn[l], w_proj_pool[l], w_out[l])
        kp_l.append(k.reshape(bp, sp // PAGE_SIZE, PAGE_SIZE, N_HEADS, HEAD_DIM))
        vp_l.append(v.reshape(bp, sp // PAGE_SIZE, PAGE_SIZE, N_HEADS, HEAD_DIM))
        pp_l.append(pbuf)
        q, k, v, ga, u, gp, g_a, g_b = project_inputs(xs, pos_s, norm_w[l], w_in[l], q_norm_w[l], k_norm_w[l])
        attn = moba_sample(q.transpose(0, 2, 1, 3), k.transpose(0, 2, 1, 3), v.transpose(0, 2, 1, 3),
                           cache_k[l], cache_v[l], page_table)
        attn = attn.transpose(0, 2, 1, 3).reshape(bs, ss, ATTN_WIDTH)
        pool, sbuf = pool_mix(u, state_pool[l], PAST_LEN, w_pool_grp[l], pool_scale[l])
        xs = merge_branches(xs, attn, ga, pool, gp, g_a, g_b, w_proj_attn[l], w_proj_pool[l], w_out[l])
        ks_l.append(k)
        vs_l.append(v)
        ps_l.append(sbuf)
    return (xp, xs, jnp.stack(kp_l), jnp.stack(vp_l), jnp.stack(pp_l), jnp.stack(ks_l), jnp.stack(vs_l), jnp.stack(ps_l))
```

```python
import functools
import math

import jax
import jax.numpy as jnp
from jax import lax
from jax.experimental import pallas as pl
from jax.experimental.pallas import tpu as pltpu

F32 = jnp.float32
BF16 = jnp.bfloat16

N_HEADS = 8
HEAD_DIM = 64
ATTN_WIDTH = N_HEADS * HEAD_DIM
MOBA_BLOCK = 256
MOBA_TOPK = 3
POOL_WINDOWS = (2, 4, 8, 16)
POOL_GROUP_WIDTH = 128
POOL_WIDTH = len(POOL_WINDOWS) * POOL_GROUP_WIDTH
POOL_BUF = max(POOL_WINDOWS) - 1
POOL_HALO = 16
ROPE_THETA = 10000.0
RMS_EPS = 1e-6
NEG_INF = -1e30
ATTN_SCALE = HEAD_DIM ** -0.5

LANES = 128
SUBLANES = 8
VMEM_LIMIT_BYTES = 56 * 1024 * 1024

TOKEN_TILE = MOBA_BLOCK


def _resident(shape):
    return pl.BlockSpec(shape, lambda *_: (0,) * len(shape), pipeline_mode=pl.Buffered(1))


def _params(n_axes):
    return pltpu.CompilerParams(dimension_semantics=("arbitrary",) * n_axes,
                                vmem_limit_bytes=VMEM_LIMIT_BYTES)


def _silu(z):
    return z * jax.nn.sigmoid(z)


def _rms_norm_rows(x, w_row):
    ms = jnp.mean(x * x, axis=-1, keepdims=True)
    return x * lax.rsqrt(ms + RMS_EPS) * w_row


def _head_norm_rope(z, ones_bd, cos_t, sin_t, w_rows):
    m = z.shape[0]
    ssq = jnp.dot((z * z).astype(BF16), ones_bd, preferred_element_type=F32)
    r = lax.rsqrt(ssq * (1.0 / HEAD_DIM) + RMS_EPS)
    cw = cos_t * w_rows[0:1, :]
    sw = sin_t * w_rows[1:2, :]
    lane = lax.broadcasted_iota(jnp.int32, (m, LANES), 1)
    first_half = (lane % HEAD_DIM) < (HEAD_DIM // 2)
    outs = []
    for c in range(ATTN_WIDTH // LANES):
        zc = z[:, c * LANES:(c + 1) * LANES]
        partner = jnp.where(first_half,
                            pltpu.roll(zc, LANES - HEAD_DIM // 2, 1),
                            pltpu.roll(zc, HEAD_DIM // 2, 1))
        outs.append((zc * cw + partner * sw) * r[:, c * LANES:(c + 1) * LANES])
    return jnp.concatenate(outs, axis=-1)


def _col_ranges():
    sizes = (ATTN_WIDTH,) * 4 + (POOL_WIDTH,) * 2
    offs = [0]
    for s in sizes:
        offs.append(offs[-1] + s)
    return offs


def _pool_project(d, zgp, zgb, wgrp_ref, pscale_ref, wpb_ref):
    parts = []
    for g in range(len(POOL_WINDOWS)):
        dg = d[:, g * POOL_GROUP_WIDTH:(g + 1) * POOL_GROUP_WIDTH].astype(BF16)
        parts.append(jnp.dot(dg, wgrp_ref[g], preferred_element_type=F32))
    pool = jnp.concatenate(parts, axis=-1) * pscale_ref[...]
    pg = (pool * _silu(zgp)).astype(BF16)
    b = jnp.dot(pg, wpb_ref[...], preferred_element_type=F32)
    return jax.nn.sigmoid(zgb) * b


def _proj_prompt_kernel(x_ref, nw_ref, win_ref, ones_ref, cos_ref, sin_ref, qw_ref, kw_ref,
                        wgrp_ref, pscale_ref, wpb_ref,
                        qt_out, kb_out, kt_out, mean_out, vtf_out, vtb_out,
                        sga_out, sig_out, yb_out, plast_out, ubuf):
    t = pl.program_id(1)
    tm = x_ref.shape[1]
    d_model = x_ref.shape[2]
    offs = _col_ranges()
    h = _rms_norm_rows(x_ref[0], nw_ref[...]).astype(BF16)

    def proj(lo, hi):
        return jnp.dot(h, win_ref[:, lo:hi], preferred_element_type=F32)

    ones_bd = ones_ref[...]
    cos_t = cos_ref[...]
    sin_t = sin_ref[...]
    n_blk = tm // MOBA_BLOCK
    n_page = tm // LANES

    q = _head_norm_rope(proj(offs[0], offs[1]), ones_bd, cos_t, sin_t, qw_ref[...]) * ATTN_SCALE
    qt = q.T.astype(BF16)
    for i in range(n_blk):
        qt_out[0, i] = qt[:, i * MOBA_BLOCK:(i + 1) * MOBA_BLOCK]

    k = _head_norm_rope(proj(offs[1], offs[2]), ones_bd, cos_t, sin_t, kw_ref[...])
    kb_out[0] = k.astype(BF16)
    kt = k.T
    for i in range(n_page):
        kt_out[0, i] = kt[:, i * LANES:(i + 1) * LANES]
    for i in range(n_blk):
        mean_out[0, i] = jnp.mean(k[i * MOBA_BLOCK:(i + 1) * MOBA_BLOCK], axis=0, keepdims=True)

    vt = proj(offs[2], offs[3]).T
    for i in range(n_page):
        vtf_out[0, i] = vt[:, i * LANES:(i + 1) * LANES]
    vtb = vt.astype(BF16)
    for i in range(n_blk):
        vtb_out[0, i] = vtb[:, i * MOBA_BLOCK:(i + 1) * MOBA_BLOCK]

    sga_out[0] = _silu(proj(offs[3], offs[4])).astype(BF16)

    zu = proj(offs[4], offs[5])

    @pl.when(t == 0)
    def _():
        ubuf[0:POOL_HALO, :] = jnp.zeros((POOL_HALO, POOL_WIDTH), F32)

    ubuf[POOL_HALO:POOL_HALO + tm, :] = zu
    pos = t * tm + lax.broadcasted_iota(jnp.int32, (tm, 1), 0)
    parts = []
    for g, w in enumerate(POOL_WINDOWS):
        cols = slice(g * POOL_GROUP_WIDTH, (g + 1) * POOL_GROUP_WIDTH)
        zug = zu[:, cols]
        acc = zug
        for back in range(1, w):
            acc = acc + ubuf[pl.ds(POOL_HALO - back, tm), cols]
        cnt = jnp.minimum(w, pos + 1).astype(F32)
        parts.append(acc / cnt - zug)
    d = jnp.concatenate(parts, axis=-1)
    ubuf[0:POOL_HALO, :] = zu[tm - POOL_HALO:tm, :]

    @pl.when(t == pl.num_programs(1) - 1)
    def _():
        plast_out[0] = zu[tm - POOL_BUF:tm, :]

    zgp = proj(offs[5], offs[6])
    zgb = proj(offs[6] + d_model, offs[6] + 2 * d_model)
    yb_out[0] = _pool_project(d, zgp, zgb, wgrp_ref, pscale_ref, wpb_ref).astype(BF16)
    sig_out[0] = jax.nn.sigmoid(proj(offs[6], offs[6] + d_model)).astype(BF16)


def _proj_prompt(x, nw, win, ones_bd, cos_t, sin_t, qw, kw, wgrp, pscale, wpb):
    bsz, seq, d_model = x.shape
    tm = TOKEN_TILE
    assert seq % tm == 0 and tm % MOBA_BLOCK == 0
    n_t = seq // tm
    n_blk, n_page = seq // MOBA_BLOCK, seq // LANES
    tile = lambda width: pl.BlockSpec((1, tm, width), lambda b, t: (b, t, 0))
    paged = lambda per, minor: pl.BlockSpec((1, per, ATTN_WIDTH, minor), lambda b, t: (b, t, 0, 0))
    out_shape = (
        jax.ShapeDtypeStruct((bsz, n_blk, ATTN_WIDTH, MOBA_BLOCK), BF16),
        jax.ShapeDtypeStruct((bsz, seq, ATTN_WIDTH), BF16),
        jax.ShapeDtypeStruct((bsz, n_page, ATTN_WIDTH, LANES), F32),
        jax.ShapeDtypeStruct((bsz, n_blk, 1, ATTN_WIDTH), F32),
        jax.ShapeDtypeStruct((bsz, n_page, ATTN_WIDTH, LANES), F32),
        jax.ShapeDtypeStruct((bsz, n_blk, ATTN_WIDTH, MOBA_BLOCK), BF16),
        jax.ShapeDtypeStruct((bsz, seq, ATTN_WIDTH), BF16),
        jax.ShapeDtypeStruct((bsz, seq, d_model), BF16),
        jax.ShapeDtypeStruct((bsz, seq, d_model), BF16),
        jax.ShapeDtypeStruct((bsz, POOL_BUF, POOL_WIDTH), F32),
    )
    out_specs = (
        paged(tm // MOBA_BLOCK, MOBA_BLOCK),
        tile(ATTN_WIDTH),
        paged(tm // LANES, LANES),
        pl.BlockSpec((1, tm // MOBA_BLOCK, 1, ATTN_WIDTH), lambda b, t: (b, t, 0, 0)),
        paged(tm // LANES, LANES),
        paged(tm // MOBA_BLOCK, MOBA_BLOCK),
        tile(ATTN_WIDTH),
        tile(d_model),
        tile(d_model),
        pl.BlockSpec((1, POOL_BUF, POOL_WIDTH), lambda b, t: (b, 0, 0)),
    )
    in_specs = [
        tile(d_model),
        _resident(nw.shape), _resident(win.shape), _resident(ones_bd.shape),
        pl.BlockSpec((tm, LANES), lambda b, t: (t, 0)),
        pl.BlockSpec((tm, LANES), lambda b, t: (t, 0)),
        _resident(qw.shape), _resident(kw.shape),
        _resident(wgrp.shape), _resident(pscale.shape), _resident(wpb.shape),
    ]
    return pl.pallas_call(
        _proj_prompt_kernel,
        grid=(bsz, n_t),
        in_specs=in_specs,
        out_specs=out_specs,
        out_shape=out_shape,
        scratch_shapes=[pltpu.VMEM((POOL_HALO + tm, POOL_WIDTH), F32)],
        compiler_params=_params(2),
        name="proj_prompt",
    )(x, nw, win, ones_bd, cos_t, sin_t, qw, kw, wgrp, pscale, wpb)


def _select_bias(sc, n_valid, n_blk):
    jrow = lax.broadcasted_iota(jnp.int32, sc.shape, 0)
    jrow_f = jrow.astype(F32)
    valid = jrow < n_valid
    s = jnp.where(valid, sc, -jnp.inf)
    for _ in range(MOBA_TOPK):
        top = jnp.max(s, axis=0, keepdims=True)
        first = jnp.min(jnp.where(s == top, jrow_f, float(n_blk)), axis=0, keepdims=True)
        s = jnp.where(jrow_f == first, -jnp.inf, s)
    return jnp.where(valid & (s == -jnp.inf), 0.0, NEG_INF)


def _attn_prompt_kernel(qt_ref, kb_ref, vtb_ref, mean_ref, o_ref, bias_scr, ot_scr):
    i = pl.program_id(1)
    n_blk = mean_ref.shape[1]
    blk = MOBA_BLOCK
    qt = qt_ref[0, 0]

    means = mean_ref[0].astype(BF16)
    tiled = jnp.concatenate([means] * N_HEADS, axis=0)
    row_head = lax.broadcasted_iota(jnp.int32, tiled.shape, 0) // n_blk
    col_head = lax.broadcasted_iota(jnp.int32, tiled.shape, 1) // HEAD_DIM
    means_bd = jnp.where(row_head == col_head, tiled, jnp.zeros_like(tiled))
    sc = jnp.dot(means_bd, qt, preferred_element_type=F32)
    for h in range(N_HEADS):
        bias_scr[h * n_blk:(h + 1) * n_blk, :] = _select_bias(sc[h * n_blk:(h + 1) * n_blk], i, n_blk)

    key_i = lax.broadcasted_iota(jnp.int32, (blk, blk), 0)
    qry_i = lax.broadcasted_iota(jnp.int32, (blk, blk), 1)
    causal = key_i <= qry_i
    pair_row = lax.broadcasted_iota(jnp.int32, (2 * HEAD_DIM, blk), 0)

    for h in range(N_HEADS):
        p2 = h // 2
        cols = slice(p2 * 2 * HEAD_DIM, (p2 + 1) * 2 * HEAD_DIM)
        qpair = qt[p2 * 2 * HEAD_DIM:(p2 + 1) * 2 * HEAD_DIM, :]
        mine = (pair_row < HEAD_DIM) if h % 2 == 0 else (pair_row >= HEAD_DIM)
        qh = jnp.where(mine, qpair, jnp.zeros_like(qpair))
        rows = slice(h * HEAD_DIM, (h + 1) * HEAD_DIM)

        def scores(j):
            start = pl.multiple_of(j * blk, blk)
            return jnp.dot(kb_ref[0, pl.ds(start, blk), cols], qh, preferred_element_type=F32)

        st = jnp.where(causal, scores(i), NEG_INF)
        m0 = jnp.max(st, axis=0, keepdims=True)
        p0 = jnp.exp(st - m0)
        l0 = jnp.sum(p0, axis=0, keepdims=True)
        acc0 = jnp.dot(vtb_ref[0, i, rows, :], p0.astype(BF16), preferred_element_type=F32)

        def body(j, carry):
            m, l, acc = carry
            sj = scores(j) + bias_scr[pl.ds(h * n_blk + j, 1), :]
            m_new = jnp.maximum(m, jnp.max(sj, axis=0, keepdims=True))
            alpha = jnp.exp(m - m_new)
            p = jnp.exp(sj - m_new)
            l_new = alpha * l + jnp.sum(p, axis=0, keepdims=True)
            pv = jnp.dot(vtb_ref[0, j, rows, :], p.astype(BF16), preferred_element_type=F32)
            return m_new, l_new, alpha * acc + pv

        _, l, acc = lax.fori_loop(0, i, body, (m0, l0, acc0))
        ot_scr[rows, :] = acc / l

    o_ref[0] = ot_scr[...].T.astype(BF16)


def _attn_prompt(qt, kb, vtb, means):
    bsz, n_blk, width, blk = qt.shape
    seq = kb.shape[1]
    return pl.pallas_call(
        _attn_prompt_kernel,
        grid=(bsz, n_blk),
        in_specs=[
            pl.BlockSpec((1, 1, width, blk), lambda b, i: (b, i, 0, 0)),
            pl.BlockSpec((1, seq, width), lambda b, i: (b, 0, 0), pipeline_mode=pl.Buffered(1)),
            pl.BlockSpec((1, n_blk, width, blk), lambda b, i: (b, 0, 0, 0), pipeline_mode=pl.Buffered(1)),
            pl.BlockSpec((1, n_blk, width), lambda b, i: (b, 0, 0)),
        ],
        out_specs=pl.BlockSpec((1, blk, width), lambda b, i: (b, i, 0)),
        out_shape=jax.ShapeDtypeStruct((bsz, seq, width), BF16),
        scratch_shapes=[pltpu.VMEM((N_HEADS * n_blk, blk), F32), pltpu.VMEM((width, blk), F32)],
        compiler_params=_params(2),
        name="attn_prompt",
    )(qt, kb, vtb, means)


def _merge_kernel(x_ref, attn_ref, sga_ref, sig_ref, yb_ref, wpa_ref, wo_ref, o_ref):
    gated = attn_ref[...].astype(BF16) * sga_ref[...]
    a = jnp.dot(gated, wpa_ref[...], preferred_element_type=F32)
    y = sig_ref[...].astype(F32) * a + yb_ref[...].astype(F32)
    o_ref[...] = x_ref[...] + jnp.dot(y.astype(BF16), wo_ref[...], preferred_element_type=F32)


def _merge(x, attn, sga, sig, yb, wpa, wo, tm):
    n, d_model = x.shape
    assert n % tm == 0
    tile = lambda width: pl.BlockSpec((tm, width), lambda t: (t, 0))
    return pl.pallas_call(
        _merge_kernel,
        grid=(n // tm,),
        in_specs=[tile(d_model), tile(ATTN_WIDTH), tile(ATTN_WIDTH), tile(d_model), tile(d_model),
                  _resident(wpa.shape), _resident(wo.shape)],
        out_specs=tile(d_model),
        out_shape=jax.ShapeDtypeStruct((n, d_model), F32),
        compiler_params=_params(1),
        name="merge",
    )(x, attn, sga, sig, yb, wpa, wo)


def _proj_sample_kernel(past_len, x_ref, nw_ref, win_ref, ones_ref, cos_ref, sin_ref, qw_ref, kw_ref,
                        wgrp_ref, pscale_ref, wpb_ref, state_ref,
                        q_out, k_out, v_out, kst_out, vst_out, sga_out, sig_out, yb_out, pool_out,
                        slab_scr):
    n_tok, d_model = x_ref.shape
    n_seq = state_ref.shape[1]
    n_new = n_tok // n_seq
    n_slab = slab_scr.shape[0]
    offs = _col_ranges()
    h = _rms_norm_rows(x_ref[...], nw_ref[...]).astype(BF16)

    def proj(lo, hi):
        return jnp.dot(h, win_ref[:, lo:hi], preferred_element_type=F32)

    def to_slabs(val):
        for c in range(n_slab):
            slab_scr[c] = val[:, c * LANES:(c + 1) * LANES]

    def step_rows(s):
        return jnp.concatenate(
            [slab_scr[c, pl.ds(s, n_seq, stride=n_new), :] for c in range(n_slab)], axis=-1)

    ones_bd = ones_ref[...]
    cos_t = cos_ref[...]
    sin_t = sin_ref[...]
    q_out[...] = _head_norm_rope(proj(offs[0], offs[1]), ones_bd, cos_t, sin_t, qw_ref[...]) * ATTN_SCALE

    k = _head_norm_rope(proj(offs[1], offs[2]), ones_bd, cos_t, sin_t, kw_ref[...])
    k_out[...] = k
    to_slabs(k)
    for s in range(n_new):
        kst_out[s] = step_rows(s).T

    v = proj(offs[2], offs[3])
    v_out[...] = v
    to_slabs(v)
    for s in range(n_new):
        vst_out[s] = step_rows(s).T

    sga_out[...] = _silu(proj(offs[3], offs[4])).astype(BF16)

    zu = proj(offs[4], offs[5])
    to_slabs(zu)
    hist = [state_ref[j] for j in range(POOL_BUF)] + [step_rows(s) for s in range(n_new)]
    for j in range(POOL_BUF):
        pool_out[j] = hist[len(hist) - POOL_BUF + j]
    for s in range(n_new):
        cur = POOL_BUF + s
        parts = []
        for g, w in enumerate(POOL_WINDOWS):
            cols = slice(g * POOL_GROUP_WIDTH, (g + 1) * POOL_GROUP_WIDTH)
            acc = hist[cur][:, cols]
            for back in range(1, w):
                acc = acc + hist[cur - back][:, cols]
            cnt = float(min(w, past_len + s + 1))
            parts.append(acc / cnt - hist[cur][:, cols])
        ds = jnp.concatenate(parts, axis=-1)
        for c in range(n_slab):
            slab_scr[c, pl.ds(s, n_seq, stride=n_new), :] = ds[:, c * LANES:(c + 1) * LANES]
    d = jnp.concatenate([slab_scr[c] for c in range(n_slab)], axis=-1)

    zgp = proj(offs[5], offs[6])
    zgb = proj(offs[6] + d_model, offs[6] + 2 * d_model)
    yb_out[...] = _pool_project(d, zgp, zgb, wgrp_ref, pscale_ref, wpb_ref).astype(BF16)
    sig_out[...] = jax.nn.sigmoid(proj(offs[6], offs[6] + d_model)).astype(BF16)


def _proj_sample(x, nw, win, ones_bd, cos_t, sin_t, qw, kw, wgrp, pscale, wpb, state_t, past_len):
    n_tok, d_model = x.shape
    n_hist, n_seq, pool_w = state_t.shape
    assert n_hist == POOL_BUF and pool_w == POOL_WIDTH and n_tok % n_seq == 0
    n_new = n_tok // n_seq
    assert ATTN_WIDTH == POOL_WIDTH
    n_slab = POOL_WIDTH // LANES
    full = lambda shape: pl.BlockSpec(shape, lambda t: (0,) * len(shape))
    out_shape = (
        jax.ShapeDtypeStruct((n_tok, ATTN_WIDTH), F32),
        jax.ShapeDtypeStruct((n_tok, ATTN_WIDTH), F32),
        jax.ShapeDtypeStruct((n_tok, ATTN_WIDTH), F32),
        jax.ShapeDtypeStruct((n_new, ATTN_WIDTH, n_seq), F32),
        jax.ShapeDtypeStruct((n_new, ATTN_WIDTH, n_seq), F32),
        jax.ShapeDtypeStruct((n_tok, ATTN_WIDTH), BF16),
        jax.ShapeDtypeStruct((n_tok, d_model), BF16),
        jax.ShapeDtypeStruct((n_tok, d_model), BF16),
        jax.ShapeDtypeStruct((POOL_BUF, n_seq, POOL_WIDTH), F32),
    )
    args = (x, nw, win, ones_bd, cos_t, sin_t, qw, kw, wgrp, pscale, wpb, state_t)
    return pl.pallas_call(
        functools.partial(_proj_sample_kernel, past_len),
        grid=(1,),
        in_specs=[full(a.shape) for a in args],
        out_specs=tuple(full(o.shape) for o in out_shape),
        out_shape=out_shape,
        scratch_shapes=[pltpu.VMEM((n_slab, n_tok, LANES), F32)],
        compiler_params=_params(1),
        name="proj_sample",
    )(*args)


def _attn_sample_kernel(layer, pt_ref, q_ref, k_ref, v_ref, ck_hbm, cv_hbm, o_ref, kbuf, vbuf, sem):
    b = pl.program_id(0)
    n_seq = pl.num_programs(0)
    n_pages = pt_ref.shape[1]
    page = ck_hbm.shape[3]
    past = n_pages * page
    n_full = past // MOBA_BLOCK
    n_new = q_ref.shape[0]
    n_row = N_HEADS * n_new

    def page_copies(seq, slot):
        copies = []
        for pg in range(n_pages):
            phys = pt_ref[seq, pg]
            win = pl.ds(pg * page, page)
            copies.append(pltpu.make_async_copy(ck_hbm.at[layer, phys], kbuf.at[slot, :, win], sem.at[0, slot]))
            copies.append(pltpu.make_async_copy(cv_hbm.at[layer, phys], vbuf.at[slot, :, win], sem.at[1, slot]))
        return copies

    slot = b % 2

    @pl.when(b == 0)
    def _():
        for c in page_copies(0, 0):
            c.start()

    @pl.when(b + 1 < n_seq)
    def _():
        for c in page_copies(b + 1, 1 - slot):
            c.start()

    for c in page_copies(b, slot):
        c.wait()

    q = q_ref[...]
    tiled = jnp.concatenate([q] * N_HEADS, axis=0)
    row_head = lax.broadcasted_iota(jnp.int32, tiled.shape, 0) // n_new
    col_head = lax.broadcasted_iota(jnp.int32, tiled.shape, 1) // HEAD_DIM
    head_lanes = row_head == col_head
    q_bd = jnp.where(head_lanes, tiled, 0.0).astype(BF16)

    s_past = jnp.dot(q_bd, kbuf[slot].astype(BF16), preferred_element_type=F32)
    nt_dims = (((1,), (1,)), ((), ()))
    s_new = lax.dot_general(q_bd, k_ref[...].astype(BF16), nt_dims, preferred_element_type=F32)
    row_step = lax.broadcasted_iota(jnp.int32, s_new.shape, 0) % n_new
    col_step = lax.broadcasted_iota(jnp.int32, s_new.shape, 1)
    s_new = jnp.where(col_step <= row_step, s_new, NEG_INF)

    blocks = [s_past[:, n * MOBA_BLOCK:(n + 1) * MOBA_BLOCK] for n in range(n_full)]
    score = [jnp.sum(sb, axis=1, keepdims=True) for sb in blocks]
    n_sel = min(MOBA_TOPK, n_full)
    masked = []
    for n in range(n_full):
        rank = jnp.zeros_like(score[n])
        for o in range(n_full):
            if o == n:
                continue
            ahead = (score[o] >= score[n]) if o < n else (score[o] > score[n])
            rank = rank + ahead.astype(F32)
        masked.append(blocks[n] + jnp.where(rank < n_sel, 0.0, NEG_INF))

    m = jnp.max(s_new, axis=1, keepdims=True)
    for sb in masked:
        m = jnp.maximum(m, jnp.max(sb, axis=1, keepdims=True))
    p_new = jnp.exp(s_new - m)
    l = jnp.sum(p_new, axis=1, keepdims=True)
    probs = []
    for sb in masked:
        pb = jnp.exp(sb - m)
        l = l + jnp.sum(pb, axis=1, keepdims=True)
        probs.append(pb.astype(BF16))
    p_past = jnp.concatenate(probs, axis=1)
    out = lax.dot_general(p_past, vbuf[slot].astype(BF16), nt_dims, preferred_element_type=F32)
    out = out + jnp.dot(p_new.astype(BF16), v_ref[...].astype(BF16), preferred_element_type=F32)
    out = jnp.where(head_lanes, out / l, 0.0)
    res = out[0:n_new]
    for hh in range(1, N_HEADS):
        res = res + out[hh * n_new:(hh + 1) * n_new]
    o_ref[...] = res


def _attn_sample(layer, page_table, q, k, v, ck_t, cv_t):
    n_seq, n_pages = page_table.shape
    n_tok, width = q.shape
    n_new = n_tok // n_seq
    page = ck_t.shape[3]
    past = n_pages * page
    assert past % MOBA_BLOCK == 0 and past >= MOBA_BLOCK
    assert ck_t.shape[2] == width
    tile = pl.BlockSpec((n_new, width), lambda b, pt: (b, 0))
    grid_spec = pltpu.PrefetchScalarGridSpec(
        num_scalar_prefetch=1,
        grid=(n_seq,),
        in_specs=[tile, tile, tile,
                  pl.BlockSpec(memory_space=pl.ANY), pl.BlockSpec(memory_space=pl.ANY)],
        out_specs=tile,
        scratch_shapes=[pltpu.VMEM((2, width, past), F32), pltpu.VMEM((2, width, past), F32),
                        pltpu.SemaphoreType.DMA((2, 2))],
    )
    return pl.pallas_call(
        functools.partial(_attn_sample_kernel, layer),
        grid_spec=grid_spec,
        out_shape=jax.ShapeDtypeStruct((n_tok, width), F32),
        compiler_params=_params(1),
        name="attn_sample",
    )(page_table, q, k, v, ck_t, cv_t)


def _rope_tables(pos):
    half = HEAD_DIM // 2
    inv_freq = jnp.exp(-math.log(ROPE_THETA) * jnp.arange(half, dtype=F32) / half)
    ang = pos.astype(F32)[:, None] * inv_freq[None, :]
    cos, sin = jnp.cos(ang), jnp.sin(ang)
    reps = LANES // HEAD_DIM
    return (jnp.tile(jnp.concatenate([cos, cos], axis=-1), (1, reps)),
            jnp.tile(jnp.concatenate([-sin, sin], axis=-1), (1, reps)))


def _norm_rows(w):
    half = HEAD_DIM // 2
    reps = LANES // HEAD_DIM
    swapped = jnp.concatenate([w[half:], w[:half]])
    return jnp.stack([jnp.tile(w, reps), jnp.tile(swapped, reps)]).astype(F32)


def kernel(x_prompt, x_sample, cache_k, cache_v, state_pool, page_table, norm_w, w_in, q_norm_w,
           k_norm_w, w_pool_grp, pool_scale, w_proj_attn, w_proj_pool, w_out):
    bp, sp, d_model = x_prompt.shape
    bs, ss, _ = x_sample.shape
    depth, n_phys, page, n_heads, head_dim = cache_k.shape
    assert (n_heads, head_dim) == (N_HEADS, HEAD_DIM) and page == LANES
    n_pages = page_table.shape[1]
    past_len = n_pages * page

    cos_p, sin_p = _rope_tables(jnp.arange(sp, dtype=jnp.int32))
    cos_s, sin_s = _rope_tables(past_len + jnp.arange(ss, dtype=jnp.int32))
    cos_s, sin_s = jnp.tile(cos_s, (bs, 1)), jnp.tile(sin_s, (bs, 1))
    ones_bd = jnp.kron(jnp.eye(N_HEADS, dtype=F32), jnp.ones((HEAD_DIM, HEAD_DIM), F32)).astype(BF16)

    ck_t = cache_k.transpose(0, 1, 3, 4, 2).reshape(depth, n_phys, ATTN_WIDTH, page)
    cv_t = cache_v.transpose(0, 1, 3, 4, 2).reshape(depth, n_phys, ATTN_WIDTH, page)
    state_t = state_pool.transpose(0, 2, 1, 3)

    w_in_b = w_in.astype(BF16)
    w_grp_b = w_pool_grp.astype(BF16)
    w_pa_b = w_proj_attn.astype(BF16)
    w_pb_b = w_proj_pool.astype(BF16)
    w_o_b = w_out.astype(BF16)

    xp = x_prompt
    xs = x_sample.reshape(bs * ss, d_model)
    kp_l, vp_l, pp_l, ks_l, vs_l, ps_l = [], [], [], [], [], []
    for l in range(depth):
        nw = norm_w[l][None, :]
        qw, kw = _norm_rows(q_norm_w[l]), _norm_rows(k_norm_w[l])
        pscale = pool_scale[l][None, :]

        qt, kb, kt, means, vtf, vtb, sga, sig, yb, plast = _proj_prompt(
            xp, nw, w_in_b[l], ones_bd, cos_p, sin_p, qw, kw, w_grp_b[l], pscale, w_pb_b[l])
        attn = _attn_prompt(qt, kb, vtb, means.reshape(bp, -1, ATTN_WIDTH))
        flat = lambda a: a.reshape(bp * sp, a.shape[-1])
        xp = _merge(flat(xp), flat(attn), flat(sga), flat(sig), flat(yb), w_pa_b[l], w_o_b[l],
                    TOKEN_TILE).reshape(bp, sp, d_model)
        to_pages = lambda a: a.reshape(bp, sp // page, N_HEADS, HEAD_DIM, page).transpose(0, 1, 4, 2, 3)
        kp_l.append(to_pages(kt))
        vp_l.append(to_pages(vtf))
        pp_l.append(plast)

        q_s, k_s, v_s, kst, vst, sga_s, sig_s, yb_s, pool_s = _proj_sample(
            xs, nw, w_in_b[l], ones_bd, cos_s, sin_s, qw, kw, w_grp_b[l], pscale, w_pb_b[l],
            state_t[l], past_len)
        attn_s = _attn_sample(l, page_table, q_s, k_s, v_s, ck_t, cv_t)
        xs = _merge(xs, attn_s, sga_s, sig_s, yb_s, w_pa_b[l], w_o_b[l], bs * ss)
        to_steps = lambda a: a.reshape(ss, N_HEADS, HEAD_DIM, bs).transpose(3, 0, 1, 2)
        ks_l.append(to_steps(kst))
        vs_l.append(to_steps(vst))
        ps_l.append(pool_s.transpose(1, 0, 2))

    return (xp, xs.reshape(bs, ss, d_model), jnp.stack(kp_l), jnp.stack(vp_l), jnp.stack(pp_l),
            jnp.stack(ks_l), jnp.stack(vs_l), jnp.stack(ps_l))
```

```python
import functools
import math

import jax
import jax.numpy as jnp
from jax import lax
from jax.experimental import pallas as pl
from jax.experimental.pallas import tpu as pltpu

F32 = jnp.float32
BF16 = jnp.bfloat16

N_HEADS = 8
HEAD_DIM = 64
ATTN_WIDTH = N_HEADS * HEAD_DIM
MOBA_BLOCK = 256
MOBA_TOPK = 3
POOL_WINDOWS = (2, 4, 8, 16)
POOL_GROUP_WIDTH = 128
POOL_WIDTH = len(POOL_WINDOWS) * POOL_GROUP_WIDTH
POOL_BUF = max(POOL_WINDOWS) - 1
POOL_HALO = 16
ROPE_THETA = 10000.0
RMS_EPS = 1e-6
NEG_INF = -1e30
ATTN_SCALE = HEAD_DIM ** -0.5
LOG2_E = math.log2(math.e)

LANES = 128
SUBLANES = 8
VMEM_LIMIT_BYTES = 56 * 1024 * 1024

TOKEN_TILE = MOBA_BLOCK


def _resident(shape):
    return pl.BlockSpec(shape, lambda *_: (0,) * len(shape), pipeline_mode=pl.Buffered(1))


def _params(n_axes):
    return pltpu.CompilerParams(dimension_semantics=("arbitrary",) * n_axes,
                                vmem_limit_bytes=VMEM_LIMIT_BYTES)


def _silu(z):
    return z * jax.nn.sigmoid(z)


def _rms_norm_rows(x, w_row):
    ms = jnp.mean(x * x, axis=-1, keepdims=True)
    return x * lax.rsqrt(ms + RMS_EPS) * w_row


def _head_norm_rope(z, ones_bd, cos_t, sin_t, w_rows):
    m = z.shape[0]
    ssq = jnp.dot((z * z).astype(BF16), ones_bd, preferred_element_type=F32)
    r = lax.rsqrt(ssq * (1.0 / HEAD_DIM) + RMS_EPS)
    cw = cos_t * w_rows[0:1, :]
    sw = sin_t * w_rows[1:2, :]
    lane = lax.broadcasted_iota(jnp.int32, (m, LANES), 1)
    first_half = (lane % HEAD_DIM) < (HEAD_DIM // 2)
    outs = []
    for c in range(ATTN_WIDTH // LANES):
        zc = z[:, c * LANES:(c + 1) * LANES]
        partner = jnp.where(first_half,
                            pltpu.roll(zc, LANES - HEAD_DIM // 2, 1),
                            pltpu.roll(zc, HEAD_DIM // 2, 1))
        outs.append((zc * cw + partner * sw) * r[:, c * LANES:(c + 1) * LANES])
    return jnp.concatenate(outs, axis=-1)


def _col_ranges():
    sizes = (ATTN_WIDTH,) * 4 + (POOL_WIDTH,) * 2
    offs = [0]
    for s in sizes:
        offs.append(offs[-1] + s)
    return offs


def _pool_project(d, zgp, zgb, wgrp_ref, pscale_ref, wpb_ref):
    parts = []
    for g in range(len(POOL_WINDOWS)):
        dg = d[:, g * POOL_GROUP_WIDTH:(g + 1) * POOL_GROUP_WIDTH].astype(BF16)
        parts.append(jnp.dot(dg, wgrp_ref[g], preferred_element_type=F32))
    pool = jnp.concatenate(parts, axis=-1) * pscale_ref[...]
    pg = (pool * _silu(zgp)).astype(BF16)
    b = jnp.dot(pg, wpb_ref[...], preferred_element_type=F32)
    return jax.nn.sigmoid(zgb) * b


def _proj_prompt_kernel(x_ref, nw_ref, win_ref, ones_ref, cos_ref, sin_ref, qw_ref, kw_ref,
                        wgrp_ref, pscale_ref, wpb_ref,
                        qt_out, kb_out, kt_out, mean_out, vtf_out, vtb_out,
                        sga_out, sig_out, yb_out, plast_out, ubuf):
    t = pl.program_id(1)
    tm = x_ref.shape[1]
    d_model = x_ref.shape[2]
    offs = _col_ranges()
    h = _rms_norm_rows(x_ref[0], nw_ref[...]).astype(BF16)

    def proj(lo, hi):
        return jnp.dot(h, win_ref[:, lo:hi], preferred_element_type=F32)

    ones_bd = ones_ref[...]
    cos_t = cos_ref[...]
    sin_t = sin_ref[...]
    n_blk = tm // MOBA_BLOCK
    n_page = tm // LANES

    q = _head_norm_rope(proj(offs[0], offs[1]), ones_bd, cos_t, sin_t, qw_ref[...]) * (ATTN_SCALE * LOG2_E)
    qt = q.T.astype(BF16)
    for i in range(n_blk):
        qt_out[0, i] = qt[:, i * MOBA_BLOCK:(i + 1) * MOBA_BLOCK]

    k = _head_norm_rope(proj(offs[1], offs[2]), ones_bd, cos_t, sin_t, kw_ref[...])
    kb_out[0] = k.astype(BF16)
    kt = k.T
    for i in range(n_page):
        kt_out[0, i] = kt[:, i * LANES:(i + 1) * LANES]
    for i in range(n_blk):
        mean_out[0, i] = jnp.mean(k[i * MOBA_BLOCK:(i + 1) * MOBA_BLOCK], axis=0, keepdims=True)

    vt = proj(offs[2], offs[3]).T
    for i in range(n_page):
        vtf_out[0, i] = vt[:, i * LANES:(i + 1) * LANES]
    vtb = vt.astype(BF16)
    for i in range(n_blk):
        vtb_out[0, i] = vtb[:, i * MOBA_BLOCK:(i + 1) * MOBA_BLOCK]

    sga_out[0] = _silu(proj(offs[3], offs[4])).astype(BF16)

    zu = proj(offs[4], offs[5])

    @pl.when(t == 0)
    def _():
        ubuf[0:POOL_HALO, :] = jnp.zeros((POOL_HALO, POOL_WIDTH), F32)

    ubuf[POOL_HALO:POOL_HALO + tm, :] = zu
    pos = t * tm + lax.broadcasted_iota(jnp.int32, (tm, 1), 0)
    parts = []
    for g, w in enumerate(POOL_WINDOWS):
        cols = slice(g * POOL_GROUP_WIDTH, (g + 1) * POOL_GROUP_WIDTH)
        zug = zu[:, cols]
        acc = zug
        for back in range(1, w):
            acc = acc + ubuf[pl.ds(POOL_HALO - back, tm), cols]
        cnt = jnp.minimum(w, pos + 1).astype(F32)
        parts.append(acc / cnt - zug)
    d = jnp.concatenate(parts, axis=-1)
    ubuf[0:POOL_HALO, :] = zu[tm - POOL_HALO:tm, :]

    @pl.when(t == pl.num_programs(1) - 1)
    def _():
        plast_out[0] = zu[tm - POOL_BUF:tm, :]

    zgp = proj(offs[5], offs[6])
    zgb = proj(offs[6] + d_model, offs[6] + 2 * d_model)
    yb_out[0] = _pool_project(d, zgp, zgb, wgrp_ref, pscale_ref, wpb_ref).astype(BF16)
    sig_out[0] = jax.nn.sigmoid(proj(offs[6], offs[6] + d_model)).astype(BF16)


def _proj_prompt(x, nw, win, ones_bd, cos_t, sin_t, qw, kw, wgrp, pscale, wpb):
    bsz, seq, d_model = x.shape
    tm = TOKEN_TILE
    assert seq % tm == 0 and tm % MOBA_BLOCK == 0
    n_t = seq // tm
    n_blk, n_page = seq // MOBA_BLOCK, seq // LANES
    tile = lambda width: pl.BlockSpec((1, tm, width), lambda b, t: (b, t, 0))
    paged = lambda per, minor: pl.BlockSpec((1, per, ATTN_WIDTH, minor), lambda b, t: (b, t, 0, 0))
    out_shape = (
        jax.ShapeDtypeStruct((bsz, n_blk, ATTN_WIDTH, MOBA_BLOCK), BF16),
        jax.ShapeDtypeStruct((bsz, seq, ATTN_WIDTH), BF16),
        jax.ShapeDtypeStruct((bsz, n_page, ATTN_WIDTH, LANES), F32),
        jax.ShapeDtypeStruct((bsz, n_blk, 1, ATTN_WIDTH), F32),
        jax.ShapeDtypeStruct((bsz, n_page, ATTN_WIDTH, LANES), F32),
        jax.ShapeDtypeStruct((bsz, n_blk, ATTN_WIDTH, MOBA_BLOCK), BF16),
        jax.ShapeDtypeStruct((bsz, seq, ATTN_WIDTH), BF16),
        jax.ShapeDtypeStruct((bsz, seq, d_model), BF16),
        jax.ShapeDtypeStruct((bsz, seq, d_model), BF16),
        jax.ShapeDtypeStruct((bsz, POOL_BUF, POOL_WIDTH), F32),
    )
    out_specs = (
        paged(tm // MOBA_BLOCK, MOBA_BLOCK),
        tile(ATTN_WIDTH),
        paged(tm // LANES, LANES),
        pl.BlockSpec((1, tm // MOBA_BLOCK, 1, ATTN_WIDTH), lambda b, t: (b, t, 0, 0)),
        paged(tm // LANES, LANES),
        paged(tm // MOBA_BLOCK, MOBA_BLOCK),
        tile(ATTN_WIDTH),
        tile(d_model),
        tile(d_model),
        pl.BlockSpec((1, POOL_BUF, POOL_WIDTH), lambda b, t: (b, 0, 0)),
    )
    in_specs = [
        tile(d_model),
        _resident(nw.shape), _resident(win.shape), _resident(ones_bd.shape),
        pl.BlockSpec((tm, LANES), lambda b, t: (t, 0)),
        pl.BlockSpec((tm, LANES), lambda b, t: (t, 0)),
        _resident(qw.shape), _resident(kw.shape),
        _resident(wgrp.shape), _resident(pscale.shape), _resident(wpb.shape),
    ]
    return pl.pallas_call(
        _proj_prompt_kernel,
        grid=(bsz, n_t),
        in_specs=in_specs,
        out_specs=out_specs,
        out_shape=out_shape,
        scratch_shapes=[pltpu.VMEM((POOL_HALO + tm, POOL_WIDTH), F32)],
        compiler_params=_params(2),
        name="proj_prompt",
    )(x, nw, win, ones_bd, cos_t, sin_t, qw, kw, wgrp, pscale, wpb)


def _select_bias(sc, n_valid, n_blk):
    jrow = lax.broadcasted_iota(jnp.int32, sc.shape, 0)
    jrow_f = jrow.astype(F32)
    valid = jrow < n_valid
    s = jnp.where(valid, sc, -jnp.inf)
    for _ in range(MOBA_TOPK):
        top = jnp.max(s, axis=0, keepdims=True)
        first = jnp.min(jnp.where(s == top, jrow_f, float(n_blk)), axis=0, keepdims=True)
        s = jnp.where(jrow_f == first, -jnp.inf, s)
    return jnp.where(valid & (s == -jnp.inf), 0.0, NEG_INF)


def _attn_prompt_kernel(qt_ref, kb_ref, vtb_ref, mean_ref, o_ref, qaug_scr, sa_scr, sb_scr, m_scr, l_scr,
                        acc_scr):
    i = pl.program_id(1)
    n_blk = mean_ref.shape[1]
    blk = MOBA_BLOCK
    pair_w = 2 * HEAD_DIM
    qt = qt_ref[0, 0]

    means = mean_ref[0].astype(BF16)
    tiled = jnp.concatenate([means] * N_HEADS, axis=0)
    row_head = lax.broadcasted_iota(jnp.int32, tiled.shape, 0) // n_blk
    col_head = lax.broadcasted_iota(jnp.int32, tiled.shape, 1) // HEAD_DIM
    means_bd = jnp.where(row_head == col_head, tiled, jnp.zeros_like(tiled))
    sc = jnp.dot(means_bd, qt, preferred_element_type=F32)

    pair_row = lax.broadcasted_iota(jnp.int32, (pair_w, blk), 0)
    blk_row = lax.broadcasted_iota(jnp.int32, (n_blk, blk), 0)
    tail_row = lax.broadcasted_iota(jnp.int32, (pair_w - n_blk, blk), 0)
    tail = jnp.where(tail_row == 0, NEG_INF, 0.0)
    for h in range(N_HEADS):
        bias = _select_bias(sc[h * n_blk:(h + 1) * n_blk], i, n_blk)
        bias = jnp.where(blk_row == i, 0.0, bias)
        qpair = qt[(h // 2) * pair_w:(h // 2 + 1) * pair_w, :]
        mine = (pair_row < HEAD_DIM) if h % 2 == 0 else (pair_row >= HEAD_DIM)
        qh = jnp.where(mine, qpair, jnp.zeros_like(qpair))
        qaug_scr[h] = jnp.concatenate([qh, jnp.concatenate([bias, tail], axis=0).astype(BF16)], axis=0)

    lane_blk = lax.broadcasted_iota(jnp.int32, (blk, pair_w), 1)

    def keys_aug(j, bias_row):
        start = pl.multiple_of(j * blk, blk)
        onehot = jnp.where(lane_blk == bias_row, 1.0, 0.0).astype(BF16)
        return [jnp.concatenate([kb_ref[0, pl.ds(start, blk), p * pair_w:(p + 1) * pair_w], onehot], axis=1)
                for p in range(N_HEADS // 2)]

    def scores(kj, h):
        return jnp.dot(kj[h // 2], qaug_scr[h], preferred_element_type=F32)

    def first_block(h, st, v_blk):
        rows = slice(h * HEAD_DIM, (h + 1) * HEAD_DIM)
        m0 = jnp.max(st, axis=0, keepdims=True)
        p0 = jnp.exp2(st - m0)
        m_scr[h] = m0
        l_scr[h] = jnp.sum(p0, axis=0, keepdims=True)
        acc_scr[rows, :] = jnp.dot(vtb_ref[0, v_blk, rows, :], p0.astype(BF16), preferred_element_type=F32)

    def next_block(h, sj, v_blk):
        rows = slice(h * HEAD_DIM, (h + 1) * HEAD_DIM)
        m_old = m_scr[h]
        m_new = jnp.maximum(m_old, jnp.max(sj, axis=0, keepdims=True))
        alpha = jnp.exp2(m_old - m_new)
        p = jnp.exp2(sj - m_new)
        l_scr[h] = alpha * l_scr[h] + jnp.sum(p, axis=0, keepdims=True)
        pv = jnp.dot(vtb_ref[0, v_blk, rows, :], p.astype(BF16), preferred_element_type=F32)
        acc_scr[rows, :] = alpha * acc_scr[rows, :] + pv
        m_scr[h] = m_new

    key_i = lax.broadcasted_iota(jnp.int32, (blk, blk), 0)
    qry_i = lax.broadcasted_iota(jnp.int32, (blk, blk), 1)
    causal = key_i <= qry_i
    ka = keys_aug(i, i)
    for h in range(N_HEADS):
        sb_scr[h] = scores(ka, h)
    k0 = keys_aug(0, 0)
    for h in range(N_HEADS):
        sa_scr[h] = scores(k0, h)
        first_block(h, jnp.where(causal, sb_scr[h], NEG_INF), i)

    def body(t, carry):
        a = 2 * t
        b = a + 1
        b_keys = jnp.minimum(b, i - 1)
        kb_aug = keys_aug(b_keys, jnp.where(b < i, b, n_blk))
        for h in range(N_HEADS):
            sb_scr[h] = scores(kb_aug, h)
            next_block(h, sa_scr[h], a)
        a_next = jnp.minimum(a + 2, i - 1)
        ka_aug = keys_aug(a_next, a_next)
        for h in range(N_HEADS):
            sa_scr[h] = scores(ka_aug, h)
            next_block(h, sb_scr[h], b_keys)
        return carry

    lax.fori_loop(0, (i + 1) // 2, body, 0)

    for h in range(N_HEADS):
        rows = slice(h * HEAD_DIM, (h + 1) * HEAD_DIM)
        acc_scr[rows, :] = acc_scr[rows, :] / l_scr[h]
    o_ref[0] = acc_scr[...].T.astype(BF16)


def _attn_prompt(qt, kb, vtb, means):
    bsz, n_blk, width, blk = qt.shape
    seq = kb.shape[1]
    assert n_blk < 2 * HEAD_DIM
    return pl.pallas_call(
        _attn_prompt_kernel,
        grid=(bsz, n_blk),
        in_specs=[
            pl.BlockSpec((1, 1, width, blk), lambda b, i: (b, i, 0, 0)),
            pl.BlockSpec((1, seq, width), lambda b, i: (b, 0, 0), pipeline_mode=pl.Buffered(1)),
            pl.BlockSpec((1, n_blk, width, blk), lambda b, i: (b, 0, 0, 0), pipeline_mode=pl.Buffered(1)),
            pl.BlockSpec((1, n_blk, width), lambda b, i: (b, 0, 0)),
        ],
        out_specs=pl.BlockSpec((1, blk, width), lambda b, i: (b, i, 0)),
        out_shape=jax.ShapeDtypeStruct((bsz, seq, width), BF16),
        scratch_shapes=[pltpu.VMEM((N_HEADS, 4 * HEAD_DIM, blk), BF16),
                        pltpu.VMEM((N_HEADS, blk, blk), F32),
                        pltpu.VMEM((N_HEADS, blk, blk), F32),
                        pltpu.VMEM((N_HEADS, 1, blk), F32),
                        pltpu.VMEM((N_HEADS, 1, blk), F32),
                        pltpu.VMEM((width, blk), F32)],
        compiler_params=_params(2),
        name="attn_prompt",
    )(qt, kb, vtb, means)


def _merge_kernel(x_ref, attn_ref, sga_ref, sig_ref, yb_ref, wpa_ref, wo_ref, o_ref):
    gated = attn_ref[...].astype(BF16) * sga_ref[...]
    a = jnp.dot(gated, wpa_ref[...], preferred_element_type=F32)
    y = sig_ref[...].astype(F32) * a + yb_ref[...].astype(F32)
    o_ref[...] = x_ref[...] + jnp.dot(y.astype(BF16), wo_ref[...], preferred_element_type=F32)


def _merge(x, attn, sga, sig, yb, wpa, wo, tm):
    n, d_model = x.shape
    assert n % tm == 0
    tile = lambda width: pl.BlockSpec((tm, width), lambda t: (t, 0))
    return pl.pallas_call(
        _merge_kernel,
        grid=(n // tm,),
        in_specs=[tile(d_model), tile(ATTN_WIDTH), tile(ATTN_WIDTH), tile(d_model), tile(d_model),
                  _resident(wpa.shape), _resident(wo.shape)],
        out_specs=tile(d_model),
        out_shape=jax.ShapeDtypeStruct((n, d_model), F32),
        compiler_params=_params(1),
        name="merge",
    )(x, attn, sga, sig, yb, wpa, wo)


def _proj_sample_kernel(past_len, x_ref, nw_ref, win_ref, ones_ref, cos_ref, sin_ref, qw_ref, kw_ref,
                        wgrp_ref, pscale_ref, wpb_ref, state_ref,
                        q_out, k_out, v_out, kst_out, vst_out, sga_out, sig_out, yb_out, pool_out,
                        slab_scr):
    n_tok, d_model = x_ref.shape
    n_seq = state_ref.shape[1]
    n_new = n_tok // n_seq
    n_slab = slab_scr.shape[0]
    offs = _col_ranges()
    h = _rms_norm_rows(x_ref[...], nw_ref[...]).astype(BF16)

    def proj(lo, hi):
        return jnp.dot(h, win_ref[:, lo:hi], preferred_element_type=F32)

    def to_slabs(val):
        for c in range(n_slab):
            slab_scr[c] = val[:, c * LANES:(c + 1) * LANES]

    def step_rows(s):
        return jnp.concatenate(
            [slab_scr[c, pl.ds(s, n_seq, stride=n_new), :] for c in range(n_slab)], axis=-1)

    ones_bd = ones_ref[...]
    cos_t = cos_ref[...]
    sin_t = sin_ref[...]
    q_out[...] = _head_norm_rope(proj(offs[0], offs[1]), ones_bd, cos_t, sin_t, qw_ref[...]) * ATTN_SCALE

    k = _head_norm_rope(proj(offs[1], offs[2]), ones_bd, cos_t, sin_t, kw_ref[...])
    k_out[...] = k
    to_slabs(k)
    for s in range(n_new):
        kst_out[s] = step_rows(s).T

    v = proj(offs[2], offs[3])
    v_out[...] = v
    to_slabs(v)
    for s in range(n_new):
        vst_out[s] = step_rows(s).T

    sga_out[...] = _silu(proj(offs[3], offs[4])).astype(BF16)

    zu = proj(offs[4], offs[5])
    to_slabs(zu)
    hist = [state_ref[j] for j in range(POOL_BUF)] + [step_rows(s) for s in range(n_new)]
    for j in range(POOL_BUF):
        pool_out[j] = hist[len(hist) - POOL_BUF + j]
    for s in range(n_new):
        cur = POOL_BUF + s
        parts = []
        for g, w in enumerate(POOL_WINDOWS):
            cols = slice(g * POOL_GROUP_WIDTH, (g + 1) * POOL_GROUP_WIDTH)
            acc = hist[cur][:, cols]
            for back in range(1, w):
                acc = acc + hist[cur - back][:, cols]
            cnt = float(min(w, past_len + s + 1))
            parts.append(acc / cnt - hist[cur][:, cols])
        ds = jnp.concatenate(parts, axis=-1)
        for c in range(n_slab):
            slab_scr[c, pl.ds(s, n_seq, stride=n_new), :] = ds[:, c * LANES:(c + 1) * LANES]
    d = jnp.concatenate([slab_scr[c] for c in range(n_slab)], axis=-1)

    zgp = proj(offs[5], offs[6])
    zgb = proj(offs[6] + d_model, offs[6] + 2 * d_model)
    yb_out[...] = _pool_project(d, zgp, zgb, wgrp_ref, pscale_ref, wpb_ref).astype(BF16)
    sig_out[...] = jax.nn.sigmoid(proj(offs[6], offs[6] + d_model)).astype(BF16)


def _proj_sample(x, nw, win, ones_bd, cos_t, sin_t, qw, kw, wgrp, pscale, wpb, state_t, past_len):
    n_tok, d_model = x.shape
    n_hist, n_seq, pool_w = state_t.shape
    assert n_hist == POOL_BUF and pool_w == POOL_WIDTH and n_tok % n_seq == 0
    n_new = n_tok // n_seq
    assert ATTN_WIDTH == POOL_WIDTH
    n_slab = POOL_WIDTH // LANES
    full = lambda shape: pl.BlockSpec(shape, lambda t: (0,) * len(shape))
    out_shape = (
        jax.ShapeDtypeStruct((n_tok, ATTN_WIDTH), F32),
        jax.ShapeDtypeStruct((n_tok, ATTN_WIDTH), F32),
        jax.ShapeDtypeStruct((n_tok, ATTN_WIDTH), F32),
        jax.ShapeDtypeStruct((n_new, ATTN_WIDTH, n_seq), F32),
        jax.ShapeDtypeStruct((n_new, ATTN_WIDTH, n_seq), F32),
        jax.ShapeDtypeStruct((n_tok, ATTN_WIDTH), BF16),
        jax.ShapeDtypeStruct((n_tok, d_model), BF16),
        jax.ShapeDtypeStruct((n_tok, d_model), BF16),
        jax.ShapeDtypeStruct((POOL_BUF, n_seq, POOL_WIDTH), F32),
    )
    args = (x, nw, win, ones_bd, cos_t, sin_t, qw, kw, wgrp, pscale, wpb, state_t)
    return pl.pallas_call(
        functools.partial(_proj_sample_kernel, past_len),
        grid=(1,),
        in_specs=[full(a.shape) for a in args],
        out_specs=tuple(full(o.shape) for o in out_shape),
        out_shape=out_shape,
        scratch_shapes=[pltpu.VMEM((n_slab, n_tok, LANES), F32)],
        compiler_params=_params(1),
        name="proj_sample",
    )(*args)


def _attn_sample_kernel(layer, pt_ref, q_ref, k_ref, v_ref, ck_hbm, cv_hbm, o_ref, kbuf, vbuf, sem):
    b = pl.program_id(0)
    n_seq = pl.num_programs(0)
    n_pages = pt_ref.shape[1]
    page = ck_hbm.shape[3]
    past = n_pages * page
    n_full = past // MOBA_BLOCK
    n_new = q_ref.shape[0]
    n_row = N_HEADS * n_new

    def page_copies(seq, slot):
        copies = []
        for pg in range(n_pages):
            phys = pt_ref[seq, pg]
            win = pl.ds(pg * page, page)
            copies.append(pltpu.make_async_copy(ck_hbm.at[layer, phys], kbuf.at[slot, :, win], sem.at[0, slot]))
            copies.append(pltpu.make_async_copy(cv_hbm.at[layer, phys], vbuf.at[slot, :, win], sem.at[1, slot]))
        return copies

    slot = b % 2

    @pl.when(b == 0)
    def _():
        for c in page_copies(0, 0):
            c.start()

    @pl.when(b + 1 < n_seq)
    def _():
        for c in page_copies(b + 1, 1 - slot):
            c.start()

    for c in page_copies(b, slot):
        c.wait()

    q = q_ref[...]
    tiled = jnp.concatenate([q] * N_HEADS, axis=0)
    row_head = lax.broadcasted_iota(jnp.int32, tiled.shape, 0) // n_new
    col_head = lax.broadcasted_iota(jnp.int32, tiled.shape, 1) // HEAD_DIM
    head_lanes = row_head == col_head
    q_bd = jnp.where(head_lanes, tiled, 0.0).astype(BF16)

    s_past = jnp.dot(q_bd, kbuf[slot].astype(BF16), preferred_element_type=F32)
    nt_dims = (((1,), (1,)), ((), ()))
    s_new = lax.dot_general(q_bd, k_ref[...].astype(BF16), nt_dims, preferred_element_type=F32)
    row_step = lax.broadcasted_iota(jnp.int32, s_new.shape, 0) % n_new
    col_step = lax.broadcasted_iota(jnp.int32, s_new.shape, 1)
    s_new = jnp.where(col_step <= row_step, s_new, NEG_INF)

    blocks = [s_past[:, n * MOBA_BLOCK:(n + 1) * MOBA_BLOCK] for n in range(n_full)]
    score = [jnp.sum(sb, axis=1, keepdims=True) for sb in blocks]
    n_sel = min(MOBA_TOPK, n_full)
    masked = []
    for n in range(n_full):
        rank = jnp.zeros_like(score[n])
        for o in range(n_full):
            if o == n:
                continue
            ahead = (score[o] >= score[n]) if o < n else (score[o] > score[n])
            rank = rank + ahead.astype(F32)
        masked.append(blocks[n] + jnp.where(rank < n_sel, 0.0, NEG_INF))

    m = jnp.max(s_new, axis=1, keepdims=True)
    for sb in masked:
        m = jnp.maximum(m, jnp.max(sb, axis=1, keepdims=True))
    p_new = jnp.exp(s_new - m)
    l = jnp.sum(p_new, axis=1, keepdims=True)
    probs = []
    for sb in masked:
        pb = jnp.exp(sb - m)
        l = l + jnp.sum(pb, axis=1, keepdims=True)
        probs.append(pb.astype(BF16))
    p_past = jnp.concatenate(probs, axis=1)
    out = lax.dot_general(p_past, vbuf[slot].astype(BF16), nt_dims, preferred_element_type=F32)
    out = out + jnp.dot(p_new.astype(BF16), v_ref[...].astype(BF16), preferred_element_type=F32)
    out = jnp.where(head_lanes, out / l, 0.0)
    res = out[0:n_new]
    for hh in range(1, N_HEADS):
        res = res + out[hh * n_new:(hh + 1) * n_new]
    o_ref[...] = res


def _attn_sample(layer, page_table, q, k, v, ck_t, cv_t):
    n_seq, n_pages = page_table.shape
    n_tok, width = q.shape
    n_new = n_tok // n_seq
    page = ck_t.shape[3]
    past = n_pages * page
    assert past % MOBA_BLOCK == 0 and past >= MOBA_BLOCK
    assert ck_t.shape[2] == width
    tile = pl.BlockSpec((n_new, width), lambda b, pt: (b, 0))
    grid_spec = pltpu.PrefetchScalarGridSpec(
        num_scalar_prefetch=1,
        grid=(n_seq,),
        in_specs=[tile, tile, tile,
                  pl.BlockSpec(memory_space=pl.ANY), pl.BlockSpec(memory_space=pl.ANY)],
        out_specs=tile,
        scratch_shapes=[pltpu.VMEM((2, width, past), F32), pltpu.VMEM((2, width, past), F32),
                        pltpu.SemaphoreType.DMA((2, 2))],
    )
    return pl.pallas_call(
        functools.partial(_attn_sample_kernel, layer),
        grid_spec=grid_spec,
        out_shape=jax.ShapeDtypeStruct((n_tok, width), F32),
        compiler_params=_params(1),
        name="attn_sample",
    )(page_table, q, k, v, ck_t, cv_t)


def _rope_tables(pos):
    half = HEAD_DIM // 2
    inv_freq = jnp.exp(-math.log(ROPE_THETA) * jnp.arange(half, dtype=F32) / half)
    ang = pos.astype(F32)[:, None] * inv_freq[None, :]
    cos, sin = jnp.cos(ang), jnp.sin(ang)
    reps = LANES // HEAD_DIM
    return (jnp.tile(jnp.concatenate([cos, cos], axis=-1), (1, reps)),
            jnp.tile(jnp.concatenate([-sin, sin], axis=-1), (1, reps)))


def _norm_rows(w):
    half = HEAD_DIM // 2
    reps = LANES // HEAD_DIM
    swapped = jnp.concatenate([w[half:], w[:half]])
    return jnp.stack([jnp.tile(w, reps), jnp.tile(swapped, reps)]).astype(F32)


def kernel(x_prompt, x_sample, cache_k, cache_v, state_pool, page_table, norm_w, w_in, q_norm_w,
           k_norm_w, w_pool_grp, pool_scale, w_proj_attn, w_proj_pool, w_out):
    bp, sp, d_model = x_prompt.shape
    bs, ss, _ = x_sample.shape
    depth, n_phys, page, n_heads, head_dim = cache_k.shape
    assert (n_heads, head_dim) == (N_HEADS, HEAD_DIM) and page == LANES
    n_pages = page_table.shape[1]
    past_len = n_pages * page

    cos_p, sin_p = _rope_tables(jnp.arange(sp, dtype=jnp.int32))
    cos_s, sin_s = _rope_tables(past_len + jnp.arange(ss, dtype=jnp.int32))
    cos_s, sin_s = jnp.tile(cos_s, (bs, 1)), jnp.tile(sin_s, (bs, 1))
    ones_bd = jnp.kron(jnp.eye(N_HEADS, dtype=F32), jnp.ones((HEAD_DIM, HEAD_DIM), F32)).astype(BF16)

    ck_t = cache_k.transpose(0, 1, 3, 4, 2).reshape(depth, n_phys, ATTN_WIDTH, page)
    cv_t = cache_v.transpose(0, 1, 3, 4, 2).reshape(depth, n_phys, ATTN_WIDTH, page)
    state_t = state_pool.transpose(0, 2, 1, 3)

    w_in_b = w_in.astype(BF16)
    w_grp_b = w_pool_grp.astype(BF16)
    w_pa_b = w_proj_attn.astype(BF16)
    w_pb_b = w_proj_pool.astype(BF16)
    w_o_b = w_out.astype(BF16)

    xp = x_prompt
    xs = x_sample.reshape(bs * ss, d_model)
    kp_l, vp_l, pp_l, ks_l, vs_l, ps_l = [], [], [], [], [], []
    for l in range(depth):
        nw = norm_w[l][None, :]
        qw, kw = _norm_rows(q_norm_w[l]), _norm_rows(k_norm_w[l])
        pscale = pool_scale[l][None, :]

        qt, kb, kt, means, vtf, vtb, sga, sig, yb, plast = _proj_prompt(
            xp, nw, w_in_b[l], ones_bd, cos_p, sin_p, qw, kw, w_grp_b[l], pscale, w_pb_b[l])
        attn = _attn_prompt(qt, kb, vtb, means.reshape(bp, -1, ATTN_WIDTH))
        flat = lambda a: a.reshape(bp * sp, a.shape[-1])
        xp = _merge(flat(xp), flat(attn), flat(sga), flat(sig), flat(yb), w_pa_b[l], w_o_b[l],
                    TOKEN_TILE).reshape(bp, sp, d_model)
        to_pages = lambda a: a.reshape(bp, sp // page, N_HEADS, HEAD_DIM, page).transpose(0, 1, 4, 2, 3)
        kp_l.append(to_pages(kt))
        vp_l.append(to_pages(vtf))
        pp_l.append(plast)

        q_s, k_s, v_s, kst, vst, sga_s, sig_s, yb_s, pool_s = _proj_sample(
            xs, nw, w_in_b[l], ones_bd, cos_s, sin_s, qw, kw, w_grp_b[l], pscale, w_pb_b[l],
            state_t[l], past_len)
        attn_s = _attn_sample(l, page_table, q_s, k_s, v_s, ck_t, cv_t)
        xs = _merge(xs, attn_s, sga_s, sig_s, yb_s, w_pa_b[l], w_o_b[l], bs * ss)
        to_steps = lambda a: a.reshape(ss, N_HEADS, HEAD_DIM, bs).transpose(3, 0, 1, 2)
        ks_l.append(to_steps(kst))
        vs_l.append(to_steps(vst))
        ps_l.append(pool_s.transpose(1, 0, 2))

    return (xp, xs.reshape(bs, ss, d_model), jnp.stack(kp_l), jnp.stack(vp_l), jnp.stack(pp_l),
            jnp.stack(ks_l), jnp.stack(vs_l), jnp.stack(ps_l))
```

```python
import functools
import math

import jax
import jax.numpy as jnp
from jax import lax
from jax.experimental import pallas as pl
from jax.experimental.pallas import tpu as pltpu

F32 = jnp.float32
BF16 = jnp.bfloat16

N_HEADS = 8
HEAD_DIM = 64
ATTN_WIDTH = N_HEADS * HEAD_DIM
MOBA_BLOCK = 256
MOBA_TOPK = 3
POOL_WINDOWS = (2, 4, 8, 16)
POOL_GROUP_WIDTH = 128
POOL_WIDTH = len(POOL_WINDOWS) * POOL_GROUP_WIDTH
POOL_BUF = max(POOL_WINDOWS) - 1
POOL_HALO = 16
ROPE_THETA = 10000.0
RMS_EPS = 1e-6
NEG_INF = -1e30
ATTN_SCALE = HEAD_DIM ** -0.5
LOG2_E = math.log2(math.e)
BF16_SUBLANES = 16
ACC_ROWS = HEAD_DIM + BF16_SUBLANES

LANES = 128
SUBLANES = 8
VMEM_LIMIT_BYTES = 56 * 1024 * 1024

PROJ_TILE = 2 * MOBA_BLOCK
MERGE_TILE = 4 * MOBA_BLOCK
KV_UNROLL = 4


def _resident(shape):
    return pl.BlockSpec(shape, lambda *_: (0,) * len(shape), pipeline_mode=pl.Buffered(1))


def _params(n_axes):
    return pltpu.CompilerParams(dimension_semantics=("arbitrary",) * n_axes,
                                vmem_limit_bytes=VMEM_LIMIT_BYTES)


def _silu(z):
    return z * jax.nn.sigmoid(z)


def _rms_norm_rows(x, w_row):
    ms = jnp.mean(x * x, axis=-1, keepdims=True)
    return x * lax.rsqrt(ms + RMS_EPS) * w_row


def _head_norm_rope(z, ones_bd, cos_t, sin_t, w_rows):
    m = z.shape[0]
    ssq = jnp.dot((z * z).astype(BF16), ones_bd, preferred_element_type=F32)
    r = lax.rsqrt(ssq * (1.0 / HEAD_DIM) + RMS_EPS)
    cw = cos_t * w_rows[0:1, :]
    sw = sin_t * w_rows[1:2, :]
    lane = lax.broadcasted_iota(jnp.int32, (m, LANES), 1)
    first_half = (lane % HEAD_DIM) < (HEAD_DIM // 2)
    outs = []
    for c in range(ATTN_WIDTH // LANES):
        zc = z[:, c * LANES:(c + 1) * LANES]
        partner = jnp.where(first_half,
                            pltpu.roll(zc, LANES - HEAD_DIM // 2, 1),
                            pltpu.roll(zc, HEAD_DIM // 2, 1))
        outs.append((zc * cw + partner * sw) * r[:, c * LANES:(c + 1) * LANES])
    return jnp.concatenate(outs, axis=-1)


def _col_ranges():
    sizes = (ATTN_WIDTH,) * 4 + (POOL_WIDTH,) * 2
    offs = [0]
    for s in sizes:
        offs.append(offs[-1] + s)
    return offs


def _pool_project(d, zgp, zgb, wgrp_ref, pscale_ref, wpb_ref):
    parts = []
    for g in range(len(POOL_WINDOWS)):
        dg = d[:, g * POOL_GROUP_WIDTH:(g + 1) * POOL_GROUP_WIDTH].astype(BF16)
        parts.append(jnp.dot(dg, wgrp_ref[g], preferred_element_type=F32))
    pool = jnp.concatenate(parts, axis=-1) * pscale_ref[...]
    pg = (pool * _silu(zgp)).astype(BF16)
    b = jnp.dot(pg, wpb_ref[...], preferred_element_type=F32)
    return jax.nn.sigmoid(zgb) * b


N_PROJ_PROMPT_OUT = 10


def _proj_prompt_kernel(x_ref, nw_ref, win_ref, ones_ref, cos_ref, sin_ref, qw_ref, kw_ref,
                        wgrp_ref, pscale_ref, wpb_ref, *rest):
    (qt_out, kb_out, kt_out, mean_out, vtf_out, vtb_out,
     sga_out, sig_out, yb_out, plast_out, ubuf) = rest[-(N_PROJ_PROMPT_OUT + 1):]
    earlier = rest[:-(N_PROJ_PROMPT_OUT + 1)]
    own = len(earlier) // 2
    for slot in range(own):
        kt_out[slot] = earlier[2 * slot][0]
        vtf_out[slot] = earlier[2 * slot + 1][0]
    t = pl.program_id(1)
    tm = x_ref.shape[1]
    d_model = x_ref.shape[2]
    offs = _col_ranges()
    h = _rms_norm_rows(x_ref[0], nw_ref[...]).astype(BF16)

    def proj(lo, hi):
        return jnp.dot(h, win_ref[:, lo:hi], preferred_element_type=F32)

    ones_bd = ones_ref[...]
    cos_t = cos_ref[...]
    sin_t = sin_ref[...]
    n_blk = tm // MOBA_BLOCK
    n_page = tm // LANES

    q = _head_norm_rope(proj(offs[0], offs[1]), ones_bd, cos_t, sin_t, qw_ref[...]) * (ATTN_SCALE * LOG2_E)
    qt = q.T.astype(BF16)
    for i in range(n_blk):
        qt_out[0, i] = qt[:, i * MOBA_BLOCK:(i + 1) * MOBA_BLOCK]

    k = _head_norm_rope(proj(offs[1], offs[2]), ones_bd, cos_t, sin_t, kw_ref[...])
    kb_out[0] = k.astype(BF16)
    kt = k.T
    for i in range(n_page):
        kt_out[own, 0, i] = kt[:, i * LANES:(i + 1) * LANES]
    for i in range(n_blk):
        mean_out[0, i] = jnp.mean(k[i * MOBA_BLOCK:(i + 1) * MOBA_BLOCK], axis=0, keepdims=True)

    vt = proj(offs[2], offs[3]).T
    for i in range(n_page):
        vtf_out[own, 0, i] = vt[:, i * LANES:(i + 1) * LANES]
    vtb = vt.astype(BF16)
    for i in range(n_blk):
        vtb_out[0, i] = vtb[:, i * MOBA_BLOCK:(i + 1) * MOBA_BLOCK]

    sga_out[0] = _silu(proj(offs[3], offs[4])).astype(BF16)

    zu = proj(offs[4], offs[5])

    @pl.when(t == 0)
    def _():
        ubuf[0:POOL_HALO, :] = jnp.zeros((POOL_HALO, POOL_WIDTH), F32)

    ubuf[POOL_HALO:POOL_HALO + tm, :] = zu
    pos = t * tm + lax.broadcasted_iota(jnp.int32, (tm, 1), 0)
    parts = []
    for g, w in enumerate(POOL_WINDOWS):
        cols = slice(g * POOL_GROUP_WIDTH, (g + 1) * POOL_GROUP_WIDTH)
        zug = zu[:, cols]
        acc = zug
        for back in range(1, w):
            acc = acc + ubuf[pl.ds(POOL_HALO - back, tm), cols]
        cnt = jnp.minimum(w, pos + 1).astype(F32)
        parts.append(acc / cnt - zug)
    d = jnp.concatenate(parts, axis=-1)
    ubuf[0:POOL_HALO, :] = zu[tm - POOL_HALO:tm, :]

    @pl.when(t == pl.num_programs(1) - 1)
    def _():
        plast_out[0] = zu[tm - POOL_BUF:tm, :]

    zgp = proj(offs[5], offs[6])
    zgb = proj(offs[6] + d_model, offs[6] + 2 * d_model)
    yb_out[0] = _pool_project(d, zgp, zgb, wgrp_ref, pscale_ref, wpb_ref).astype(BF16)
    sig_out[0] = jax.nn.sigmoid(proj(offs[6], offs[6] + d_model)).astype(BF16)


def _proj_prompt(earlier_pages, x, nw, win, ones_bd, cos_t, sin_t, qw, kw, wgrp, pscale, wpb):
    bsz, seq, d_model = x.shape
    tm = PROJ_TILE
    assert seq % tm == 0 and tm % MOBA_BLOCK == 0
    n_t = seq // tm
    n_blk, n_page = seq // MOBA_BLOCK, seq // LANES
    n_slots = len(earlier_pages) // 2 + 1
    tile = lambda width: pl.BlockSpec((1, tm, width), lambda b, t: (b, t, 0))
    paged = lambda per, minor: pl.BlockSpec((1, per, ATTN_WIDTH, minor), lambda b, t: (b, t, 0, 0))
    pages = lambda slots: pl.BlockSpec((slots, 1, tm // LANES, ATTN_WIDTH, LANES), lambda b, t: (0, b, t, 0, 0))
    out_shape = (
        jax.ShapeDtypeStruct((bsz, n_blk, ATTN_WIDTH, MOBA_BLOCK), BF16),
        jax.ShapeDtypeStruct((bsz, seq, ATTN_WIDTH), BF16),
        jax.ShapeDtypeStruct((n_slots, bsz, n_page, ATTN_WIDTH, LANES), F32),
        jax.ShapeDtypeStruct((bsz, n_blk, 1, ATTN_WIDTH), F32),
        jax.ShapeDtypeStruct((n_slots, bsz, n_page, ATTN_WIDTH, LANES), F32),
        jax.ShapeDtypeStruct((bsz, n_blk, ATTN_WIDTH, MOBA_BLOCK), BF16),
        jax.ShapeDtypeStruct((bsz, seq, ATTN_WIDTH), BF16),
        jax.ShapeDtypeStruct((bsz, seq, d_model), BF16),
        jax.ShapeDtypeStruct((bsz, seq, d_model), BF16),
        jax.ShapeDtypeStruct((bsz, POOL_BUF, POOL_WIDTH), F32),
    )
    out_specs = (
        paged(tm // MOBA_BLOCK, MOBA_BLOCK),
        tile(ATTN_WIDTH),
        pages(n_slots),
        pl.BlockSpec((1, tm // MOBA_BLOCK, 1, ATTN_WIDTH), lambda b, t: (b, t, 0, 0)),
        pages(n_slots),
        paged(tm // MOBA_BLOCK, MOBA_BLOCK),
        tile(ATTN_WIDTH),
        tile(d_model),
        tile(d_model),
        pl.BlockSpec((1, POOL_BUF, POOL_WIDTH), lambda b, t: (b, 0, 0)),
    )
    in_specs = [
        tile(d_model),
        _resident(nw.shape), _resident(win.shape), _resident(ones_bd.shape),
        pl.BlockSpec((tm, LANES), lambda b, t: (t, 0)),
        pl.BlockSpec((tm, LANES), lambda b, t: (t, 0)),
        _resident(qw.shape), _resident(kw.shape),
        _resident(wgrp.shape), _resident(pscale.shape), _resident(wpb.shape),
    ]
    in_specs += [pages(1)] * len(earlier_pages)
    args = [x, nw, win, ones_bd, cos_t, sin_t, qw, kw, wgrp, pscale, wpb, *earlier_pages]
    return pl.pallas_call(
        _proj_prompt_kernel,
        grid=(bsz, n_t),
        in_specs=in_specs,
        out_specs=out_specs,
        out_shape=out_shape,
        scratch_shapes=[pltpu.VMEM((POOL_HALO + tm, POOL_WIDTH), F32)],
        compiler_params=_params(2),
        name="proj_prompt",
    )(*args)


def _select_bias(sc, n_valid, n_blk):
    jrow = lax.broadcasted_iota(jnp.int32, sc.shape, 0)
    jrow_f = jrow.astype(F32)
    valid = jrow < n_valid
    s = jnp.where(valid, sc, -jnp.inf)
    for _ in range(MOBA_TOPK):
        top = jnp.max(s, axis=0, keepdims=True)
        first = jnp.min(jnp.where(s == top, jrow_f, float(n_blk)), axis=0, keepdims=True)
        s = jnp.where(jrow_f == first, -jnp.inf, s)
    return jnp.where(valid & (s == -jnp.inf), 0.0, NEG_INF)


def _attn_prompt_kernel(qt_ref, kb_ref, vtb_ref, mean_ref, o_ref, qaug_scr, sa_scr, sb_scr, m_scr, acc_scr):
    i = pl.program_id(1)
    n_blk = mean_ref.shape[1]
    blk = MOBA_BLOCK
    pair_w = 2 * HEAD_DIM
    qt = qt_ref[0, 0]

    means = mean_ref[0].astype(BF16)
    tiled = jnp.concatenate([means] * N_HEADS, axis=0)
    row_head = lax.broadcasted_iota(jnp.int32, tiled.shape, 0) // n_blk
    col_head = lax.broadcasted_iota(jnp.int32, tiled.shape, 1) // HEAD_DIM
    means_bd = jnp.where(row_head == col_head, tiled, jnp.zeros_like(tiled))
    sc = jnp.dot(means_bd, qt, preferred_element_type=F32)

    pair_row = lax.broadcasted_iota(jnp.int32, (pair_w, blk), 0)
    blk_row = lax.broadcasted_iota(jnp.int32, (n_blk, blk), 0)
    tail_row = lax.broadcasted_iota(jnp.int32, (pair_w - n_blk, blk), 0)
    tail = jnp.where(tail_row == 0, NEG_INF, 0.0)
    for h in range(N_HEADS):
        bias = _select_bias(sc[h * n_blk:(h + 1) * n_blk], i, n_blk)
        bias = jnp.where(blk_row == i, 0.0, bias)
        qpair = qt[(h // 2) * pair_w:(h // 2 + 1) * pair_w, :]
        mine = (pair_row < HEAD_DIM) if h % 2 == 0 else (pair_row >= HEAD_DIM)
        qh = jnp.where(mine, qpair, jnp.zeros_like(qpair))
        qaug_scr[h] = jnp.concatenate([qh, jnp.concatenate([bias, tail], axis=0).astype(BF16)], axis=0)

    lane_blk = lax.broadcasted_iota(jnp.int32, (blk, pair_w), 1)

    def keys_aug(j, bias_row):
        start = pl.multiple_of(j * blk, blk)
        onehot = jnp.where(lane_blk == bias_row, 1.0, 0.0).astype(BF16)
        return [jnp.concatenate([kb_ref[0, pl.ds(start, blk), p * pair_w:(p + 1) * pair_w], onehot], axis=1)
                for p in range(N_HEADS // 2)]

    def scores(kj, h):
        return jnp.dot(kj[h // 2], qaug_scr[h], preferred_element_type=F32)

    ones_rows = jnp.ones((ACC_ROWS - HEAD_DIM, blk), BF16)

    def pv_and_sum(h, v_blk, p):
        vt_h = vtb_ref[0, v_blk, h * HEAD_DIM:(h + 1) * HEAD_DIM, :]
        return jnp.dot(jnp.concatenate([vt_h, ones_rows], axis=0), p.astype(BF16), preferred_element_type=F32)

    def first_block(h, st, v_blk):
        m0 = jnp.max(st, axis=0, keepdims=True)
        m_scr[h] = m0
        acc_scr[h] = pv_and_sum(h, v_blk, jnp.exp2(st - m0))

    def next_block(h, sj, v_blk):
        m_old = m_scr[h]
        m_new = jnp.maximum(m_old, jnp.max(sj, axis=0, keepdims=True))
        acc_scr[h] = jnp.exp2(m_old - m_new) * acc_scr[h] + pv_and_sum(h, v_blk, jnp.exp2(sj - m_new))
        m_scr[h] = m_new

    key_i = lax.broadcasted_iota(jnp.int32, (blk, blk), 0)
    qry_i = lax.broadcasted_iota(jnp.int32, (blk, blk), 1)
    causal = key_i <= qry_i
    ka = keys_aug(i, i)
    for h in range(N_HEADS):
        sb_scr[h] = scores(ka, h)
    k0 = keys_aug(0, 0)
    for h in range(N_HEADS):
        sa_scr[h] = scores(k0, h)
        first_block(h, jnp.where(causal, sb_scr[h], NEG_INF), i)

    def make_body(unroll):
        def body(t, base):
            for u in range(unroll):
                cur = base + t * unroll + u
                nxt = cur + 1
                nxt_keys = jnp.minimum(nxt, i - 1)
                k_aug = keys_aug(nxt_keys, jnp.where(nxt < i, nxt, n_blk))
                read, write = (sa_scr, sb_scr) if u % 2 == 0 else (sb_scr, sa_scr)
                for h in range(N_HEADS):
                    write[h] = scores(k_aug, h)
                    next_block(h, read[h], jnp.minimum(cur, i - 1))
            return base
        return body

    n_wide = i // KV_UNROLL
    lax.fori_loop(0, n_wide, make_body(KV_UNROLL), 0)
    done = n_wide * KV_UNROLL
    lax.fori_loop(0, (i - done + 1) // 2, make_body(2), done)

    outs = []
    for h in range(N_HEADS):
        acc = acc_scr[h]
        outs.append(acc[0:HEAD_DIM] / acc[HEAD_DIM:HEAD_DIM + 1])
    o_ref[0] = jnp.concatenate(outs, axis=0).T.astype(BF16)


def _attn_prompt(qt, kb, vtb, means):
    bsz, n_blk, width, blk = qt.shape
    seq = kb.shape[1]
    assert n_blk < 2 * HEAD_DIM
    return pl.pallas_call(
        _attn_prompt_kernel,
        grid=(bsz, n_blk),
        in_specs=[
            pl.BlockSpec((1, 1, width, blk), lambda b, i: (b, i, 0, 0)),
            pl.BlockSpec((1, seq, width), lambda b, i: (b, 0, 0), pipeline_mode=pl.Buffered(1)),
            pl.BlockSpec((1, n_blk, width, blk), lambda b, i: (b, 0, 0, 0), pipeline_mode=pl.Buffered(1)),
            pl.BlockSpec((1, n_blk, width), lambda b, i: (b, 0, 0)),
        ],
        out_specs=pl.BlockSpec((1, blk, width), lambda b, i: (b, i, 0)),
        out_shape=jax.ShapeDtypeStruct((bsz, seq, width), BF16),
        scratch_shapes=[pltpu.VMEM((N_HEADS, 4 * HEAD_DIM, blk), BF16),
                        pltpu.VMEM((N_HEADS, blk, blk), F32),
                        pltpu.VMEM((N_HEADS, blk, blk), F32),
                        pltpu.VMEM((N_HEADS, 1, blk), F32),
                        pltpu.VMEM((N_HEADS, ACC_ROWS, blk), F32)],
        compiler_params=_params(2),
        name="attn_prompt",
    )(qt, kb, vtb, means)


def _merge_kernel(x_ref, attn_ref, sga_ref, sig_ref, yb_ref, wpa_ref, wo_ref, o_ref):
    gated = attn_ref[...].astype(BF16) * sga_ref[...]
    a = jnp.dot(gated, wpa_ref[...], preferred_element_type=F32)
    y = sig_ref[...].astype(F32) * a + yb_ref[...].astype(F32)
    o_ref[...] = x_ref[...] + jnp.dot(y.astype(BF16), wo_ref[...], preferred_element_type=F32)


def _merge(x, attn, sga, sig, yb, wpa, wo, tm):
    n, d_model = x.shape
    assert n % tm == 0
    tile = lambda width: pl.BlockSpec((tm, width), lambda t: (t, 0))
    return pl.pallas_call(
        _merge_kernel,
        grid=(n // tm,),
        in_specs=[tile(d_model), tile(ATTN_WIDTH), tile(ATTN_WIDTH), tile(d_model), tile(d_model),
                  _resident(wpa.shape), _resident(wo.shape)],
        out_specs=tile(d_model),
        out_shape=jax.ShapeDtypeStruct((n, d_model), F32),
        compiler_params=_params(1),
        name="merge",
    )(x, attn, sga, sig, yb, wpa, wo)


def _proj_sample_kernel(past_len, x_ref, nw_ref, win_ref, ones_ref, cos_ref, sin_ref, qw_ref, kw_ref,
                        wgrp_ref, pscale_ref, wpb_ref, state_ref,
                        q_out, k_out, v_out, kst_out, vst_out, sga_out, sig_out, yb_out, pool_out,
                        slab_scr):
    n_tok, d_model = x_ref.shape
    n_seq = state_ref.shape[1]
    n_new = n_tok // n_seq
    n_slab = slab_scr.shape[0]
    offs = _col_ranges()
    h = _rms_norm_rows(x_ref[...], nw_ref[...]).astype(BF16)

    def proj(lo, hi):
        return jnp.dot(h, win_ref[:, lo:hi], preferred_element_type=F32)

    def to_slabs(val):
        for c in range(n_slab):
            slab_scr[c] = val[:, c * LANES:(c + 1) * LANES]

    def step_rows(s):
        return jnp.concatenate(
            [slab_scr[c, pl.ds(s, n_seq, stride=n_new), :] for c in range(n_slab)], axis=-1)

    ones_bd = ones_ref[...]
    cos_t = cos_ref[...]
    sin_t = sin_ref[...]
    q_out[...] = _head_norm_rope(proj(offs[0], offs[1]), ones_bd, cos_t, sin_t, qw_ref[...]) * ATTN_SCALE

    k = _head_norm_rope(proj(offs[1], offs[2]), ones_bd, cos_t, sin_t, kw_ref[...])
    k_out[...] = k
    to_slabs(k)
    for s in range(n_new):
        kst_out[s] = step_rows(s).T

    v = proj(offs[2], offs[3])
    v_out[...] = v
    to_slabs(v)
    for s in range(n_new):
        vst_out[s] = step_rows(s).T

    sga_out[...] = _silu(proj(offs[3], offs[4])).astype(BF16)

    zu = proj(offs[4], offs[5])
    to_slabs(zu)
    hist = [state_ref[j] for j in range(POOL_BUF)] + [step_rows(s) for s in range(n_new)]
    for j in range(POOL_BUF):
        pool_out[j] = hist[len(hist) - POOL_BUF + j]
    for s in range(n_new):
        cur = POOL_BUF + s
        parts = []
        for g, w in enumerate(POOL_WINDOWS):
            cols = slice(g * POOL_GROUP_WIDTH, (g + 1) * POOL_GROUP_WIDTH)
            acc = hist[cur][:, cols]
            for back in range(1, w):
                acc = acc + hist[cur - back][:, cols]
            cnt = float(min(w, past_len + s + 1))
            parts.append(acc / cnt - hist[cur][:, cols])
        ds = jnp.concatenate(parts, axis=-1)
        for c in range(n_slab):
            slab_scr[c, pl.ds(s, n_seq, stride=n_new), :] = ds[:, c * LANES:(c + 1) * LANES]
    d = jnp.concatenate([slab_scr[c] for c in range(n_slab)], axis=-1)

    zgp = proj(offs[5], offs[6])
    zgb = proj(offs[6] + d_model, offs[6] + 2 * d_model)
    yb_out[...] = _pool_project(d, zgp, zgb, wgrp_ref, pscale_ref, wpb_ref).astype(BF16)
    sig_out[...] = jax.nn.sigmoid(proj(offs[6], offs[6] + d_model)).astype(BF16)


def _proj_sample(x, nw, win, ones_bd, cos_t, sin_t, qw, kw, wgrp, pscale, wpb, state_t, past_len):
    n_tok, d_model = x.shape
    n_hist, n_seq, pool_w = state_t.shape
    assert n_hist == POOL_BUF and pool_w == POOL_WIDTH and n_tok % n_seq == 0
    n_new = n_tok // n_seq
    assert ATTN_WIDTH == POOL_WIDTH
    n_slab = POOL_WIDTH // LANES
    full = lambda shape: pl.BlockSpec(shape, lambda t: (0,) * len(shape))
    out_shape = (
        jax.ShapeDtypeStruct((n_tok, ATTN_WIDTH), F32),
        jax.ShapeDtypeStruct((n_tok, ATTN_WIDTH), F32),
        jax.ShapeDtypeStruct((n_tok, ATTN_WIDTH), F32),
        jax.ShapeDtypeStruct((n_new, ATTN_WIDTH, n_seq), F32),
        jax.ShapeDtypeStruct((n_new, ATTN_WIDTH, n_seq), F32),
        jax.ShapeDtypeStruct((n_tok, ATTN_WIDTH), BF16),
        jax.ShapeDtypeStruct((n_tok, d_model), BF16),
        jax.ShapeDtypeStruct((n_tok, d_model), BF16),
        jax.ShapeDtypeStruct((POOL_BUF, n_seq, POOL_WIDTH), F32),
    )
    args = (x, nw, win, ones_bd, cos_t, sin_t, qw, kw, wgrp, pscale, wpb, state_t)
    return pl.pallas_call(
        functools.partial(_proj_sample_kernel, past_len),
        grid=(1,),
        in_specs=[full(a.shape) for a in args],
        out_specs=tuple(full(o.shape) for o in out_shape),
        out_shape=out_shape,
        scratch_shapes=[pltpu.VMEM((n_slab, n_tok, LANES), F32)],
        compiler_params=_params(1),
        name="proj_sample",
    )(*args)


def _attn_sample_kernel(layer, pt_ref, q_ref, k_ref, v_ref, ck_hbm, cv_hbm, o_ref, kbuf, vbuf, sem):
    b = pl.program_id(0)
    n_seq = pl.num_programs(0)
    n_pages = pt_ref.shape[1]
    page = ck_hbm.shape[3]
    past = n_pages * page
    n_full = past // MOBA_BLOCK
    n_new = q_ref.shape[0]
    n_row = N_HEADS * n_new

    def page_copies(seq, slot):
        copies = []
        for pg in range(n_pages):
            phys = pt_ref[seq, pg]
            win = pl.ds(pg * page, page)
            copies.append(pltpu.make_async_copy(ck_hbm.at[layer, phys], kbuf.at[slot, :, win], sem.at[0, slot]))
            copies.append(pltpu.make_async_copy(cv_hbm.at[layer, phys], vbuf.at[slot, :, win], sem.at[1, slot]))
        return copies

    slot = b % 2

    @pl.when(b == 0)
    def _():
        for c in page_copies(0, 0):
            c.start()

    @pl.when(b + 1 < n_seq)
    def _():
        for c in page_copies(b + 1, 1 - slot):
            c.start()

    for c in page_copies(b, slot):
        c.wait()

    q = q_ref[...]
    tiled = jnp.concatenate([q] * N_HEADS, axis=0)
    row_head = lax.broadcasted_iota(jnp.int32, tiled.shape, 0) // n_new
    col_head = lax.broadcasted_iota(jnp.int32, tiled.shape, 1) // HEAD_DIM
    head_lanes = row_head == col_head
    q_bd = jnp.where(head_lanes, tiled, 0.0).astype(BF16)

    s_past = jnp.dot(q_bd, kbuf[slot].astype(BF16), preferred_element_type=F32)
    nt_dims = (((1,), (1,)), ((), ()))
    s_new = lax.dot_general(q_bd, k_ref[...].astype(BF16), nt_dims, preferred_element_type=F32)
    row_step = lax.broadcasted_iota(jnp.int32, s_new.shape, 0) % n_new
    col_step = lax.broadcasted_iota(jnp.int32, s_new.shape, 1)
    s_new = jnp.where(col_step <= row_step, s_new, NEG_INF)

    blocks = [s_past[:, n * MOBA_BLOCK:(n + 1) * MOBA_BLOCK] for n in range(n_full)]
    score = [jnp.sum(sb, axis=1, keepdims=True) for sb in blocks]
    n_sel = min(MOBA_TOPK, n_full)
    masked = []
    for n in range(n_full):
        rank = jnp.zeros_like(score[n])
        for o in range(n_full):
            if o == n:
                continue
            ahead = (score[o] >= score[n]) if o < n else (score[o] > score[n])
            rank = rank + ahead.astype(F32)
        masked.append(blocks[n] + jnp.where(rank < n_sel, 0.0, NEG_INF))

    m = jnp.max(s_new, axis=1, keepdims=True)
    for sb in masked:
        m = jnp.maximum(m, jnp.max(sb, axis=1, keepdims=True))
    p_new = jnp.exp(s_new - m)
    l = jnp.sum(p_new, axis=1, keepdims=True)
    probs = []
    for sb in masked:
        pb = jnp.exp(sb - m)
        l = l + jnp.sum(pb, axis=1, keepdims=True)
        probs.append(pb.astype(BF16))
    p_past = jnp.concatenate(probs, axis=1)
    out = lax.dot_general(p_past, vbuf[slot].astype(BF16), nt_dims, preferred_element_type=F32)
    out = out + jnp.dot(p_new.astype(BF16), v_ref[...].astype(BF16), preferred_element_type=F32)
    out = jnp.where(head_lanes, out / l, 0.0)
    res = out[0:n_new]
    for hh in range(1, N_HEADS):
        res = res + out[hh * n_new:(hh + 1) * n_new]
    o_ref[...] = res


def _attn_sample(layer, page_table, q, k, v, ck_t, cv_t):
    n_seq, n_pages = page_table.shape
    n_tok, width = q.shape
    n_new = n_tok // n_seq
    page = ck_t.shape[3]
    past = n_pages * page
    assert past % MOBA_BLOCK == 0 and past >= MOBA_BLOCK
    assert ck_t.shape[2] == width
    tile = pl.BlockSpec((n_new, width), lambda b, pt: (b, 0))
    grid_spec = pltpu.PrefetchScalarGridSpec(
        num_scalar_prefetch=1,
        grid=(n_seq,),
        in_specs=[tile, tile, tile,
                  pl.BlockSpec(memory_space=pl.ANY), pl.BlockSpec(memory_space=pl.ANY)],
        out_specs=tile,
        scratch_shapes=[pltpu.VMEM((2, width, past), F32), pltpu.VMEM((2, width, past), F32),
                        pltpu.SemaphoreType.DMA((2, 2))],
    )
    return pl.pallas_call(
        functools.partial(_attn_sample_kernel, layer),
        grid_spec=grid_spec,
        out_shape=jax.ShapeDtypeStruct((n_tok, width), F32),
        compiler_params=_params(1),
        name="attn_sample",
    )(page_table, q, k, v, ck_t, cv_t)


def _rope_tables(pos):
    half = HEAD_DIM // 2
    inv_freq = jnp.exp(-math.log(ROPE_THETA) * jnp.arange(half, dtype=F32) / half)
    ang = pos.astype(F32)[:, None] * inv_freq[None, :]
    cos, sin = jnp.cos(ang), jnp.sin(ang)
    reps = LANES // HEAD_DIM
    return (jnp.tile(jnp.concatenate([cos, cos], axis=-1), (1, reps)),
            jnp.tile(jnp.concatenate([-sin, sin], axis=-1), (1, reps)))


def _norm_rows(w):
    half = HEAD_DIM // 2
    reps = LANES // HEAD_DIM
    swapped = jnp.concatenate([w[half:], w[:half]])
    return jnp.stack([jnp.tile(w, reps), jnp.tile(swapped, reps)]).astype(F32)


def kernel(x_prompt, x_sample, cache_k, cache_v, state_pool, page_table, norm_w, w_in, q_norm_w,
           k_norm_w, w_pool_grp, pool_scale, w_proj_attn, w_proj_pool, w_out):
    bp, sp, d_model = x_prompt.shape
    bs, ss, _ = x_sample.shape
    depth, n_phys, page, n_heads, head_dim = cache_k.shape
    assert (n_heads, head_dim) == (N_HEADS, HEAD_DIM) and page == LANES
    n_pages = page_table.shape[1]
    past_len = n_pages * page

    cos_p, sin_p = _rope_tables(jnp.arange(sp, dtype=jnp.int32))
    cos_s, sin_s = _rope_tables(past_len + jnp.arange(ss, dtype=jnp.int32))
    cos_s, sin_s = jnp.tile(cos_s, (bs, 1)), jnp.tile(sin_s, (bs, 1))
    ones_bd = jnp.kron(jnp.eye(N_HEADS, dtype=F32), jnp.ones((HEAD_DIM, HEAD_DIM), F32)).astype(BF16)

    ck_t = cache_k.transpose(0, 1, 3, 4, 2).reshape(depth, n_phys, ATTN_WIDTH, page)
    cv_t = cache_v.transpose(0, 1, 3, 4, 2).reshape(depth, n_phys, ATTN_WIDTH, page)
    state_t = state_pool.transpose(0, 2, 1, 3)

    w_in_b = w_in.astype(BF16)
    w_grp_b = w_pool_grp.astype(BF16)
    w_pa_b = w_proj_attn.astype(BF16)
    w_pb_b = w_proj_pool.astype(BF16)
    w_o_b = w_out.astype(BF16)

    xp = x_prompt
    xs = x_sample.reshape(bs * ss, d_model)
    kv_pages = []
    pp_l, ks_l, vs_l, ps_l = [], [], [], []
    for l in range(depth):
        nw = norm_w[l][None, :]
        qw, kw = _norm_rows(q_norm_w[l]), _norm_rows(k_norm_w[l])
        pscale = pool_scale[l][None, :]

        qt, kb, kt, means, vtf, vtb, sga, sig, yb, plast = _proj_prompt(
            kv_pages if l == depth - 1 else [], xp, nw, w_in_b[l], ones_bd, cos_p, sin_p, qw, kw,
            w_grp_b[l], pscale, w_pb_b[l])
        kv_pages += [kt, vtf]
        attn = _attn_prompt(qt, kb, vtb, means.reshape(bp, -1, ATTN_WIDTH))
        flat = lambda a: a.reshape(bp * sp, a.shape[-1])
        xp = _merge(flat(xp), flat(attn), flat(sga), flat(sig), flat(yb), w_pa_b[l], w_o_b[l],
                    MERGE_TILE).reshape(bp, sp, d_model)
        pp_l.append(plast)

        q_s, k_s, v_s, kst, vst, sga_s, sig_s, yb_s, pool_s = _proj_sample(
            xs, nw, w_in_b[l], ones_bd, cos_s, sin_s, qw, kw, w_grp_b[l], pscale, w_pb_b[l],
            state_t[l], past_len)
        attn_s = _attn_sample(l, page_table, q_s, k_s, v_s, ck_t, cv_t)
        xs = _merge(xs, attn_s, sga_s, sig_s, yb_s, w_pa_b[l], w_o_b[l], bs * ss)
        to_steps = lambda a: a.reshape(ss, N_HEADS, HEAD_DIM, bs).transpose(3, 0, 1, 2)
        ks_l.append(to_steps(kst))
        vs_l.append(to_steps(vst))
        ps_l.append(pool_s.transpose(1, 0, 2))

    to_pages = lambda a: a.reshape(depth, bp, sp // page, N_HEADS, HEAD_DIM, page).transpose(0, 1, 2, 5, 3, 4)
    return (xp, xs.reshape(bs, ss, d_model), to_pages(kv_pages[-2]), to_pages(kv_pages[-1]), jnp.stack(pp_l),
            jnp.stack(ks_l), jnp.stack(vs_l), jnp.stack(ps_l))
```

```python
import functools
import math

import jax
import jax.numpy as jnp
from jax import lax
from jax.experimental import pallas as pl
from jax.experimental.pallas import tpu as pltpu

F32 = jnp.float32
BF16 = jnp.bfloat16

N_HEADS = 8
HEAD_DIM = 64
ATTN_WIDTH = N_HEADS * HEAD_DIM
MOBA_BLOCK = 256
MOBA_TOPK = 3
POOL_WINDOWS = (2, 4, 8, 16)
POOL_GROUP_WIDTH = 128
POOL_WIDTH = len(POOL_WINDOWS) * POOL_GROUP_WIDTH
POOL_BUF = max(POOL_WINDOWS) - 1
POOL_HALO = 16
ROPE_THETA = 10000.0
RMS_EPS = 1e-6
NEG_INF = -1e30
ATTN_SCALE = HEAD_DIM ** -0.5
LOG2_E = math.log2(math.e)
BF16_SUBLANES = 16
ACC_ROWS = HEAD_DIM + BF16_SUBLANES

LANES = 128
SUBLANES = 8
VMEM_LIMIT_BYTES = 56 * 1024 * 1024

PROJ_TILE = 2 * MOBA_BLOCK
MERGE_TILE = 4 * MOBA_BLOCK
KV_UNROLL = 4


def _resident(shape):
    return pl.BlockSpec(shape, lambda *_: (0,) * len(shape), pipeline_mode=pl.Buffered(1))


def _params(n_axes):
    return pltpu.CompilerParams(dimension_semantics=("arbitrary",) * n_axes,
                                vmem_limit_bytes=VMEM_LIMIT_BYTES)


def _silu(z):
    return z * jax.nn.sigmoid(z)


def _rms_norm_rows(x, w_row):
    ms = jnp.mean(x * x, axis=-1, keepdims=True)
    return x * lax.rsqrt(ms + RMS_EPS) * w_row


def _head_norm_rope(z, ones_bd, cos_t, sin_t, w_rows):
    m = z.shape[0]
    ssq = jnp.dot((z * z).astype(BF16), ones_bd, preferred_element_type=F32)
    r = lax.rsqrt(ssq * (1.0 / HEAD_DIM) + RMS_EPS)
    cw = cos_t * w_rows[0:1, :]
    sw = sin_t * w_rows[1:2, :]
    lane = lax.broadcasted_iota(jnp.int32, (m, LANES), 1)
    first_half = (lane % HEAD_DIM) < (HEAD_DIM // 2)
    outs = []
    for c in range(ATTN_WIDTH // LANES):
        zc = z[:, c * LANES:(c + 1) * LANES]
        partner = jnp.where(first_half,
                            pltpu.roll(zc, LANES - HEAD_DIM // 2, 1),
                            pltpu.roll(zc, HEAD_DIM // 2, 1))
        outs.append((zc * cw + partner * sw) * r[:, c * LANES:(c + 1) * LANES])
    return jnp.concatenate(outs, axis=-1)


def _col_ranges():
    sizes = (ATTN_WIDTH,) * 4 + (POOL_WIDTH,) * 2
    offs = [0]
    for s in sizes:
        offs.append(offs[-1] + s)
    return offs


def _pool_project(d, zgp, zgb, wgrp_ref, pscale_ref, wpb_ref):
    parts = []
    for g in range(len(POOL_WINDOWS)):
        dg = d[:, g * POOL_GROUP_WIDTH:(g + 1) * POOL_GROUP_WIDTH].astype(BF16)
        parts.append(jnp.dot(dg, wgrp_ref[g], preferred_element_type=F32))
    pool = jnp.concatenate(parts, axis=-1) * pscale_ref[...]
    pg = (pool * _silu(zgp)).astype(BF16)
    b = jnp.dot(pg, wpb_ref[...], preferred_element_type=F32)
    return jax.nn.sigmoid(zgb) * b


N_PROJ_PROMPT_OUT = 10


def _proj_prompt_kernel(x_ref, nw_ref, win_ref, ones_ref, cos_ref, sin_ref, qw_ref, kw_ref,
                        wgrp_ref, pscale_ref, wpb_ref, *rest):
    (qt_out, kb_out, kt_out, mean_out, vtf_out, vtb_out,
     sga_out, sig_out, yb_out, plast_out, ubuf) = rest[-(N_PROJ_PROMPT_OUT + 1):]
    earlier = rest[:-(N_PROJ_PROMPT_OUT + 1)]
    own = len(earlier) // 2
    for slot in range(own):
        kt_out[slot] = earlier[2 * slot][0]
        vtf_out[slot] = earlier[2 * slot + 1][0]
    t = pl.program_id(1)
    tm = x_ref.shape[1]
    d_model = x_ref.shape[2]
    offs = _col_ranges()
    h = _rms_norm_rows(x_ref[0], nw_ref[...]).astype(BF16)

    def proj(lo, hi):
        return jnp.dot(h, win_ref[:, lo:hi], preferred_element_type=F32)

    ones_bd = ones_ref[...]
    cos_t = cos_ref[...]
    sin_t = sin_ref[...]
    n_blk = tm // MOBA_BLOCK
    n_page = tm // LANES

    q = _head_norm_rope(proj(offs[0], offs[1]), ones_bd, cos_t, sin_t, qw_ref[...]) * (ATTN_SCALE * LOG2_E)
    qt = q.T.astype(BF16)
    for i in range(n_blk):
        qt_out[0, i] = qt[:, i * MOBA_BLOCK:(i + 1) * MOBA_BLOCK]

    k = _head_norm_rope(proj(offs[1], offs[2]), ones_bd, cos_t, sin_t, kw_ref[...])
    kb_out[0] = k.astype(BF16)
    kt = k.T
    for i in range(n_page):
        kt_out[own, 0, i] = kt[:, i * LANES:(i + 1) * LANES]
    for i in range(n_blk):
        mean_out[0, i] = jnp.mean(k[i * MOBA_BLOCK:(i + 1) * MOBA_BLOCK], axis=0, keepdims=True)

    vt = proj(offs[2], offs[3]).T
    for i in range(n_page):
        vtf_out[own, 0, i] = vt[:, i * LANES:(i + 1) * LANES]
    vtb = vt.astype(BF16)
    for i in range(n_blk):
        vtb_out[0, i] = vtb[:, i * MOBA_BLOCK:(i + 1) * MOBA_BLOCK]

    sga_out[0] = _silu(proj(offs[3], offs[4])).astype(BF16)

    zu = proj(offs[4], offs[5])

    @pl.when(t == 0)
    def _():
        ubuf[0:POOL_HALO, :] = jnp.zeros((POOL_HALO, POOL_WIDTH), F32)

    ubuf[POOL_HALO:POOL_HALO + tm, :] = zu
    pos = t * tm + lax.broadcasted_iota(jnp.int32, (tm, 1), 0)
    parts = []
    for g, w in enumerate(POOL_WINDOWS):
        cols = slice(g * POOL_GROUP_WIDTH, (g + 1) * POOL_GROUP_WIDTH)
        zug = zu[:, cols]
        acc = zug
        for back in range(1, w):
            acc = acc + ubuf[pl.ds(POOL_HALO - back, tm), cols]
        cnt = jnp.minimum(w, pos + 1).astype(F32)
        parts.append(acc / cnt - zug)
    d = jnp.concatenate(parts, axis=-1)
    ubuf[0:POOL_HALO, :] = zu[tm - POOL_HALO:tm, :]

    @pl.when(t == pl.num_programs(1) - 1)
    def _():
        plast_out[0] = zu[tm - POOL_BUF:tm, :]

    zgp = proj(offs[5], offs[6])
    zgb = proj(offs[6] + d_model, offs[6] + 2 * d_model)
    yb_out[0] = _pool_project(d, zgp, zgb, wgrp_ref, pscale_ref, wpb_ref).astype(BF16)
    sig_out[0] = jax.nn.sigmoid(proj(offs[6], offs[6] + d_model)).astype(BF16)


def _proj_prompt(earlier_pages, x, nw, win, ones_bd, cos_t, sin_t, qw, kw, wgrp, pscale, wpb):
    bsz, seq, d_model = x.shape
    tm = PROJ_TILE
    assert seq % tm == 0 and tm % MOBA_BLOCK == 0
    n_t = seq // tm
    n_blk, n_page = seq // MOBA_BLOCK, seq // LANES
    n_slots = len(earlier_pages) // 2 + 1
    tile = lambda width: pl.BlockSpec((1, tm, width), lambda b, t: (b, t, 0))
    paged = lambda per, minor: pl.BlockSpec((1, per, ATTN_WIDTH, minor), lambda b, t: (b, t, 0, 0))
    pages = lambda slots: pl.BlockSpec((slots, 1, tm // LANES, ATTN_WIDTH, LANES), lambda b, t: (0, b, t, 0, 0))
    out_shape = (
        jax.ShapeDtypeStruct((bsz, n_blk, ATTN_WIDTH, MOBA_BLOCK), BF16),
        jax.ShapeDtypeStruct((bsz, seq, ATTN_WIDTH), BF16),
        jax.ShapeDtypeStruct((n_slots, bsz, n_page, ATTN_WIDTH, LANES), F32),
        jax.ShapeDtypeStruct((bsz, n_blk, 1, ATTN_WIDTH), F32),
        jax.ShapeDtypeStruct((n_slots, bsz, n_page, ATTN_WIDTH, LANES), F32),
        jax.ShapeDtypeStruct((bsz, n_blk, ATTN_WIDTH, MOBA_BLOCK), BF16),
        jax.ShapeDtypeStruct((bsz, seq, ATTN_WIDTH), BF16),
        jax.ShapeDtypeStruct((bsz, seq, d_model), BF16),
        jax.ShapeDtypeStruct((bsz, seq, d_model), BF16),
        jax.ShapeDtypeStruct((bsz, POOL_BUF, POOL_WIDTH), F32),
    )
    out_specs = (
        paged(tm // MOBA_BLOCK, MOBA_BLOCK),
        tile(ATTN_WIDTH),
        pages(n_slots),
        pl.BlockSpec((1, tm // MOBA_BLOCK, 1, ATTN_WIDTH), lambda b, t: (b, t, 0, 0)),
        pages(n_slots),
        paged(tm // MOBA_BLOCK, MOBA_BLOCK),
        tile(ATTN_WIDTH),
        tile(d_model),
        tile(d_model),
        pl.BlockSpec((1, POOL_BUF, POOL_WIDTH), lambda b, t: (b, 0, 0)),
    )
    in_specs = [
        tile(d_model),
        _resident(nw.shape), _resident(win.shape), _resident(ones_bd.shape),
        pl.BlockSpec((tm, LANES), lambda b, t: (t, 0)),
        pl.BlockSpec((tm, LANES), lambda b, t: (t, 0)),
        _resident(qw.shape), _resident(kw.shape),
        _resident(wgrp.shape), _resident(pscale.shape), _resident(wpb.shape),
    ]
    in_specs += [pages(1)] * len(earlier_pages)
    args = [x, nw, win, ones_bd, cos_t, sin_t, qw, kw, wgrp, pscale, wpb, *earlier_pages]
    return pl.pallas_call(
        _proj_prompt_kernel,
        grid=(bsz, n_t),
        in_specs=in_specs,
        out_specs=out_specs,
        out_shape=out_shape,
        scratch_shapes=[pltpu.VMEM((POOL_HALO + tm, POOL_WIDTH), F32)],
        compiler_params=_params(2),
        name="proj_prompt",
    )(*args)


def _select_bias(sc, n_valid, n_blk):
    jrow = lax.broadcasted_iota(jnp.int32, sc.shape, 0)
    jrow_f = jrow.astype(F32)
    valid = jrow < n_valid
    s = jnp.where(valid, sc, -jnp.inf)
    for _ in range(MOBA_TOPK):
        top = jnp.max(s, axis=0, keepdims=True)
        first = jnp.min(jnp.where(s == top, jrow_f, float(n_blk)), axis=0, keepdims=True)
        s = jnp.where(jrow_f == first, -jnp.inf, s)
    return jnp.where(valid & (s == -jnp.inf), 0.0, NEG_INF)


def _attn_prompt_tile(i, qt_ref, kb_ref, vtb_ref, mean_ref, o_ref, qaug_scr, sa_scr, sb_scr, m_scr, acc_scr):
    n_blk = mean_ref.shape[1]
    blk = MOBA_BLOCK
    pair_w = 2 * HEAD_DIM
    qt = qt_ref[0, 0]

    means = mean_ref[0].astype(BF16)
    tiled = jnp.concatenate([means] * N_HEADS, axis=0)
    row_head = lax.broadcasted_iota(jnp.int32, tiled.shape, 0) // n_blk
    col_head = lax.broadcasted_iota(jnp.int32, tiled.shape, 1) // HEAD_DIM
    means_bd = jnp.where(row_head == col_head, tiled, jnp.zeros_like(tiled))
    sc = jnp.dot(means_bd, qt, preferred_element_type=F32)

    pair_row = lax.broadcasted_iota(jnp.int32, (pair_w, blk), 0)
    blk_row = lax.broadcasted_iota(jnp.int32, (n_blk, blk), 0)
    tail_row = lax.broadcasted_iota(jnp.int32, (pair_w - n_blk, blk), 0)
    tail = jnp.where(tail_row == 0, NEG_INF, 0.0)
    for h in range(N_HEADS):
        bias = _select_bias(sc[h * n_blk:(h + 1) * n_blk], i, n_blk)
        bias = jnp.where(blk_row == i, 0.0, bias)
        qpair = qt[(h // 2) * pair_w:(h // 2 + 1) * pair_w, :]
        mine = (pair_row < HEAD_DIM) if h % 2 == 0 else (pair_row >= HEAD_DIM)
        qh = jnp.where(mine, qpair, jnp.zeros_like(qpair))
        qaug_scr[h] = jnp.concatenate([qh, jnp.concatenate([bias, tail], axis=0).astype(BF16)], axis=0)

    lane_blk = lax.broadcasted_iota(jnp.int32, (blk, pair_w), 1)

    def keys_aug(j, bias_row):
        start = pl.multiple_of(j * blk, blk)
        onehot = jnp.where(lane_blk == bias_row, 1.0, 0.0).astype(BF16)
        return [jnp.concatenate([kb_ref[0, pl.ds(start, blk), p * pair_w:(p + 1) * pair_w], onehot], axis=1)
                for p in range(N_HEADS // 2)]

    def scores(kj, h):
        return jnp.dot(kj[h // 2], qaug_scr[h], preferred_element_type=F32)

    ones_rows = jnp.ones((ACC_ROWS - HEAD_DIM, blk), BF16)

    def pv_and_sum(h, v_blk, p):
        vt_h = vtb_ref[0, v_blk, h * HEAD_DIM:(h + 1) * HEAD_DIM, :]
        return jnp.dot(jnp.concatenate([vt_h, ones_rows], axis=0), p.astype(BF16), preferred_element_type=F32)

    def first_block(h, st, v_blk):
        m0 = jnp.max(st, axis=0, keepdims=True)
        m_scr[h] = m0
        acc_scr[h] = pv_and_sum(h, v_blk, jnp.exp2(st - m0))

    def next_block(h, sj, v_blk):
        m_old = m_scr[h]
        m_new = jnp.maximum(m_old, jnp.max(sj, axis=0, keepdims=True))
        acc_scr[h] = jnp.exp2(m_old - m_new) * acc_scr[h] + pv_and_sum(h, v_blk, jnp.exp2(sj - m_new))
        m_scr[h] = m_new

    key_i = lax.broadcasted_iota(jnp.int32, (blk, blk), 0)
    qry_i = lax.broadcasted_iota(jnp.int32, (blk, blk), 1)
    causal = key_i <= qry_i
    ka = keys_aug(i, i)
    for h in range(N_HEADS):
        sb_scr[h] = scores(ka, h)
    k0 = keys_aug(0, 0)
    for h in range(N_HEADS):
        sa_scr[h] = scores(k0, h)
        first_block(h, jnp.where(causal, sb_scr[h], NEG_INF), i)

    def make_body(unroll):
        def body(t, base):
            for u in range(unroll):
                cur = base + t * unroll + u
                nxt = cur + 1
                nxt_keys = jnp.minimum(nxt, i - 1)
                k_aug = keys_aug(nxt_keys, jnp.where(nxt < i, nxt, n_blk))
                read, write = (sa_scr, sb_scr) if u % 2 == 0 else (sb_scr, sa_scr)
                for h in range(N_HEADS):
                    write[h] = scores(k_aug, h)
                    next_block(h, read[h], jnp.minimum(cur, i - 1))
            return base
        return body

    n_wide = i // KV_UNROLL
    lax.fori_loop(0, n_wide, make_body(KV_UNROLL), 0)
    done = n_wide * KV_UNROLL
    lax.fori_loop(0, (i - done + 1) // 2, make_body(2), done)

    outs = []
    for h in range(N_HEADS):
        acc = acc_scr[h]
        outs.append(acc[0:HEAD_DIM] / acc[HEAD_DIM:HEAD_DIM + 1])
    o_ref[0] = jnp.concatenate(outs, axis=0).T.astype(BF16)


def _attn_sample_seq(q, k_new, v_new, kt, vt):
    n_new = q.shape[0]
    n_full = kt.shape[1] // MOBA_BLOCK
    tiled = jnp.concatenate([q] * N_HEADS, axis=0)
    row_head = lax.broadcasted_iota(jnp.int32, tiled.shape, 0) // n_new
    col_head = lax.broadcasted_iota(jnp.int32, tiled.shape, 1) // HEAD_DIM
    head_lanes = row_head == col_head
    q_bd = jnp.where(head_lanes, tiled, 0.0).astype(BF16)

    s_past = jnp.dot(q_bd, kt, preferred_element_type=F32)
    nt_dims = (((1,), (1,)), ((), ()))
    s_new = lax.dot_general(q_bd, k_new.astype(BF16), nt_dims, preferred_element_type=F32)
    row_step = lax.broadcasted_iota(jnp.int32, s_new.shape, 0) % n_new
    col_step = lax.broadcasted_iota(jnp.int32, s_new.shape, 1)
    s_new = jnp.where(col_step <= row_step, s_new, NEG_INF)

    blocks = [s_past[:, n * MOBA_BLOCK:(n + 1) * MOBA_BLOCK] for n in range(n_full)]
    score = [jnp.sum(sb, axis=1, keepdims=True) for sb in blocks]
    n_sel = min(MOBA_TOPK, n_full)
    lane = lax.broadcasted_iota(jnp.int32, (tiled.shape[0], LANES), 1)
    by_lane = jnp.full(lane.shape, -jnp.inf, F32)
    for n in range(n_full):
        by_lane = jnp.where(lane == n, score[n], by_lane)
    masked = []
    for n in range(n_full):
        ahead = (by_lane > score[n]) | ((lane < n) & (by_lane == score[n]))
        rank = jnp.sum(jnp.where(ahead, 1.0, 0.0), axis=1, keepdims=True)
        masked.append(blocks[n] + jnp.where(rank < n_sel, 0.0, NEG_INF))

    m = jnp.max(s_new, axis=1, keepdims=True)
    for sb in masked:
        m = jnp.maximum(m, jnp.max(sb, axis=1, keepdims=True))
    p_new = jnp.exp(s_new - m)
    l = jnp.sum(p_new, axis=1, keepdims=True)
    probs = []
    for sb in masked:
        pb = jnp.exp(sb - m)
        l = l + jnp.sum(pb, axis=1, keepdims=True)
        probs.append(pb.astype(BF16))
    p_past = jnp.concatenate(probs, axis=1)
    out = lax.dot_general(p_past, vt, nt_dims, preferred_element_type=F32)
    out = out + jnp.dot(p_new.astype(BF16), v_new.astype(BF16), preferred_element_type=F32)
    out = jnp.where(head_lanes, out / l, 0.0)
    res = out[0:n_new]
    for hh in range(1, N_HEADS):
        res = res + out[hh * n_new:(hh + 1) * n_new]
    return res


def _attn_kernel(layer, pt_ref, qt_ref, kb_ref, vtb_ref, mean_ref, qs_ref, ks_ref, vs_ref, ck_hbm, cv_hbm,
                 o_ref, os_ref, qaug_scr, sa_scr, sb_scr, m_scr, acc_scr, kbuf, vbuf, sem):
    step = pl.program_id(0) * pl.num_programs(1) + pl.program_id(1)
    n_steps = pl.num_programs(0) * pl.num_programs(1)
    per_step = kbuf.shape[0]
    n_pages = pt_ref.shape[1]
    page = ck_hbm.shape[3]
    n_new = qs_ref.shape[0] // per_step

    def page_copies(seq, slot):
        copies = []
        for pg in range(n_pages):
            phys = pt_ref[seq, pg]
            win = pl.ds(pg * page, page)
            copies.append(pltpu.make_async_copy(ck_hbm.at[layer, phys], kbuf.at[slot, :, win], sem.at[0, slot]))
            copies.append(pltpu.make_async_copy(cv_hbm.at[layer, phys], vbuf.at[slot, :, win], sem.at[1, slot]))
        return copies

    @pl.when(step == 0)
    def _():
        for slot in range(per_step):
            for c in page_copies(slot, slot):
                c.start()

    for slot in range(per_step):
        for c in page_copies(step * per_step + slot, slot):
            c.wait()
    for slot in range(per_step):
        rows = slice(slot * n_new, (slot + 1) * n_new)
        os_ref[rows, :] = _attn_sample_seq(qs_ref[rows, :], ks_ref[rows, :], vs_ref[rows, :],
                                           kbuf[slot].astype(BF16), vbuf[slot].astype(BF16))

    @pl.when(step + 1 < n_steps)
    def _():
        for slot in range(per_step):
            for c in page_copies((step + 1) * per_step + slot, slot):
                c.start()

    _attn_prompt_tile(pl.program_id(1), qt_ref, kb_ref, vtb_ref, mean_ref, o_ref,
                      qaug_scr, sa_scr, sb_scr, m_scr, acc_scr)


def _attn(layer, page_table, qt, kb, vtb, means, q_s, k_s, v_s, ck_t, cv_t):
    bsz, n_blk, width, blk = qt.shape
    seq = kb.shape[1]
    assert n_blk < 2 * HEAD_DIM
    n_seq, n_pages = page_table.shape
    n_tok = q_s.shape[0]
    n_new = n_tok // n_seq
    page = ck_t.shape[3]
    past = n_pages * page
    assert past % MOBA_BLOCK == 0 and past >= MOBA_BLOCK and ck_t.shape[2] == width
    n_steps = bsz * n_blk
    assert n_seq % n_steps == 0
    per_step = n_seq // n_steps
    sample_tile = pl.BlockSpec((per_step * n_new, width), lambda b, i, pt: (b * n_blk + i, 0))
    grid_spec = pltpu.PrefetchScalarGridSpec(
        num_scalar_prefetch=1,
        grid=(bsz, n_blk),
        in_specs=[
            pl.BlockSpec((1, 1, width, blk), lambda b, i, pt: (b, i, 0, 0)),
            pl.BlockSpec((1, seq, width), lambda b, i, pt: (b, 0, 0), pipeline_mode=pl.Buffered(1)),
            pl.BlockSpec((1, n_blk, width, blk), lambda b, i, pt: (b, 0, 0, 0), pipeline_mode=pl.Buffered(1)),
            pl.BlockSpec((1, n_blk, width), lambda b, i, pt: (b, 0, 0)),
            sample_tile, sample_tile, sample_tile,
            pl.BlockSpec(memory_space=pl.ANY), pl.BlockSpec(memory_space=pl.ANY),
        ],
        out_specs=(pl.BlockSpec((1, blk, width), lambda b, i, pt: (b, i, 0)), sample_tile),
        scratch_shapes=[pltpu.VMEM((N_HEADS, 4 * HEAD_DIM, blk), BF16),
                        pltpu.VMEM((N_HEADS, blk, blk), F32),
                        pltpu.VMEM((N_HEADS, blk, blk), F32),
                        pltpu.VMEM((N_HEADS, 1, blk), F32),
                        pltpu.VMEM((N_HEADS, ACC_ROWS, blk), F32),
                        pltpu.VMEM((per_step, width, past), F32),
                        pltpu.VMEM((per_step, width, past), F32),
                        pltpu.SemaphoreType.DMA((2, per_step))],
    )
    return pl.pallas_call(
        functools.partial(_attn_kernel, layer),
        grid_spec=grid_spec,
        out_shape=(jax.ShapeDtypeStruct((bsz, seq, width), BF16), jax.ShapeDtypeStruct((n_tok, width), F32)),
        compiler_params=_params(2),
        name="attn",
    )(page_table, qt, kb, vtb, means, q_s, k_s, v_s, ck_t, cv_t)


def _merge_kernel(x_ref, attn_ref, sga_ref, sig_ref, yb_ref, wpa_ref, wo_ref, o_ref):
    gated = attn_ref[...].astype(BF16) * sga_ref[...]
    a = jnp.dot(gated, wpa_ref[...], preferred_element_type=F32)
    y = sig_ref[...].astype(F32) * a + yb_ref[...].astype(F32)
    o_ref[...] = x_ref[...] + jnp.dot(y.astype(BF16), wo_ref[...], preferred_element_type=F32)


def _merge(x, attn, sga, sig, yb, wpa, wo, tm):
    n, d_model = x.shape
    assert n % tm == 0
    tile = lambda width: pl.BlockSpec((tm, width), lambda t: (t, 0))
    return pl.pallas_call(
        _merge_kernel,
        grid=(n // tm,),
        in_specs=[tile(d_model), tile(ATTN_WIDTH), tile(ATTN_WIDTH), tile(d_model), tile(d_model),
                  _resident(wpa.shape), _resident(wo.shape)],
        out_specs=tile(d_model),
        out_shape=jax.ShapeDtypeStruct((n, d_model), F32),
        compiler_params=_params(1),
        name="merge",
    )(x, attn, sga, sig, yb, wpa, wo)


def _proj_sample_kernel(past_len, x_ref, nw_ref, win_ref, ones_ref, cos_ref, sin_ref, qw_ref, kw_ref,
                        wgrp_ref, pscale_ref, wpb_ref, state_ref,
                        q_out, k_out, v_out, kst_out, vst_out, sga_out, sig_out, yb_out, pool_out,
                        slab_scr):
    n_tok, d_model = x_ref.shape
    n_seq = state_ref.shape[1]
    n_new = n_tok // n_seq
    n_slab = slab_scr.shape[0]
    offs = _col_ranges()
    h = _rms_norm_rows(x_ref[...], nw_ref[...]).astype(BF16)

    def proj(lo, hi):
        return jnp.dot(h, win_ref[:, lo:hi], preferred_element_type=F32)

    def to_slabs(val):
        for c in range(n_slab):
            slab_scr[c] = val[:, c * LANES:(c + 1) * LANES]

    def step_rows(s):
        return jnp.concatenate(
            [slab_scr[c, pl.ds(s, n_seq, stride=n_new), :] for c in range(n_slab)], axis=-1)

    ones_bd = ones_ref[...]
    cos_t = cos_ref[...]
    sin_t = sin_ref[...]
    q_out[...] = _head_norm_rope(proj(offs[0], offs[1]), ones_bd, cos_t, sin_t, qw_ref[...]) * ATTN_SCALE

    k = _head_norm_rope(proj(offs[1], offs[2]), ones_bd, cos_t, sin_t, kw_ref[...])
    k_out[...] = k
    to_slabs(k)
    for s in range(n_new):
        kst_out[s] = step_rows(s).T

    v = proj(offs[2], offs[3])
    v_out[...] = v
    to_slabs(v)
    for s in range(n_new):
        vst_out[s] = step_rows(s).T

    sga_out[...] = _silu(proj(offs[3], offs[4])).astype(BF16)

    zu = proj(offs[4], offs[5])
    to_slabs(zu)
    hist = [state_ref[j] for j in range(POOL_BUF)] + [step_rows(s) for s in range(n_new)]
    for j in range(POOL_BUF):
        pool_out[j] = hist[len(hist) - POOL_BUF + j]
    for s in range(n_new):
        cur = POOL_BUF + s
        parts = []
        for g, w in enumerate(POOL_WINDOWS):
            cols = slice(g * POOL_GROUP_WIDTH, (g + 1) * POOL_GROUP_WIDTH)
            acc = hist[cur][:, cols]
            for back in range(1, w):
                acc = acc + hist[cur - back][:, cols]
            cnt = float(min(w, past_len + s + 1))
            parts.append(acc / cnt - hist[cur][:, cols])
        ds = jnp.concatenate(parts, axis=-1)
        for c in range(n_slab):
            slab_scr[c, pl.ds(s, n_seq, stride=n_new), :] = ds[:, c * LANES:(c + 1) * LANES]
    d = jnp.concatenate([slab_scr[c] for c in range(n_slab)], axis=-1)

    zgp = proj(offs[5], offs[6])
    zgb = proj(offs[6] + d_model, offs[6] + 2 * d_model)
    yb_out[...] = _pool_project(d, zgp, zgb, wgrp_ref, pscale_ref, wpb_ref).astype(BF16)
    sig_out[...] = jax.nn.sigmoid(proj(offs[6], offs[6] + d_model)).astype(BF16)


def _proj_sample(x, nw, win, ones_bd, cos_t, sin_t, qw, kw, wgrp, pscale, wpb, state_t, past_len):
    n_tok, d_model = x.shape
    n_hist, n_seq, pool_w = state_t.shape
    assert n_hist == POOL_BUF and pool_w == POOL_WIDTH and n_tok % n_seq == 0
    n_new = n_tok // n_seq
    assert ATTN_WIDTH == POOL_WIDTH
    n_slab = POOL_WIDTH // LANES
    full = lambda shape: pl.BlockSpec(shape, lambda t: (0,) * len(shape))
    out_shape = (
        jax.ShapeDtypeStruct((n_tok, ATTN_WIDTH), F32),
        jax.ShapeDtypeStruct((n_tok, ATTN_WIDTH), F32),
        jax.ShapeDtypeStruct((n_tok, ATTN_WIDTH), F32),
        jax.ShapeDtypeStruct((n_new, ATTN_WIDTH, n_seq), F32),
        jax.ShapeDtypeStruct((n_new, ATTN_WIDTH, n_seq), F32),
        jax.ShapeDtypeStruct((n_tok, ATTN_WIDTH), BF16),
        jax.ShapeDtypeStruct((n_tok, d_model), BF16),
        jax.ShapeDtypeStruct((n_tok, d_model), BF16),
        jax.ShapeDtypeStruct((POOL_BUF, n_seq, POOL_WIDTH), F32),
    )
    args = (x, nw, win, ones_bd, cos_t, sin_t, qw, kw, wgrp, pscale, wpb, state_t)
    return pl.pallas_call(
        functools.partial(_proj_sample_kernel, past_len),
        grid=(1,),
        in_specs=[full(a.shape) for a in args],
        out_specs=tuple(full(o.shape) for o in out_shape),
        out_shape=out_shape,
        scratch_shapes=[pltpu.VMEM((n_slab, n_tok, LANES), F32)],
        compiler_params=_params(1),
        name="proj_sample",
    )(*args)


def _rope_tables(pos):
    half = HEAD_DIM // 2
    inv_freq = jnp.exp(-math.log(ROPE_THETA) * jnp.arange(half, dtype=F32) / half)
    ang = pos.astype(F32)[:, None] * inv_freq[None, :]
    cos, sin = jnp.cos(ang), jnp.sin(ang)
    reps = LANES // HEAD_DIM
    return (jnp.tile(jnp.concatenate([cos, cos], axis=-1), (1, reps)),
            jnp.tile(jnp.concatenate([-sin, sin], axis=-1), (1, reps)))


def _norm_rows(w):
    half = HEAD_DIM // 2
    reps = LANES // HEAD_DIM
    swapped = jnp.concatenate([w[half:], w[:half]])
    return jnp.stack([jnp.tile(w, reps), jnp.tile(swapped, reps)]).astype(F32)


def kernel(x_prompt, x_sample, cache_k, cache_v, state_pool, page_table, norm_w, w_in, q_norm_w,
           k_norm_w, w_pool_grp, pool_scale, w_proj_attn, w_proj_pool, w_out):
    bp, sp, d_model = x_prompt.shape
    bs, ss, _ = x_sample.shape
    depth, n_phys, page, n_heads, head_dim = cache_k.shape
    assert (n_heads, head_dim) == (N_HEADS, HEAD_DIM) and page == LANES
    n_pages = page_table.shape[1]
    past_len = n_pages * page

    cos_p, sin_p = _rope_tables(jnp.arange(sp, dtype=jnp.int32))
    cos_s, sin_s = _rope_tables(past_len + jnp.arange(ss, dtype=jnp.int32))
    cos_s, sin_s = jnp.tile(cos_s, (bs, 1)), jnp.tile(sin_s, (bs, 1))
    ones_bd = jnp.kron(jnp.eye(N_HEADS, dtype=F32), jnp.ones((HEAD_DIM, HEAD_DIM), F32)).astype(BF16)

    ck_t = cache_k.transpose(0, 1, 3, 4, 2).reshape(depth, n_phys, ATTN_WIDTH, page)
    cv_t = cache_v.transpose(0, 1, 3, 4, 2).reshape(depth, n_phys, ATTN_WIDTH, page)
    state_t = state_pool.transpose(0, 2, 1, 3)

    w_in_b = w_in.astype(BF16)
    w_grp_b = w_pool_grp.astype(BF16)
    w_pa_b = w_proj_attn.astype(BF16)
    w_pb_b = w_proj_pool.astype(BF16)
    w_o_b = w_out.astype(BF16)

    xp = x_prompt
    xs = x_sample.reshape(bs * ss, d_model)
    kv_pages = []
    pp_l, ks_l, vs_l, ps_l = [], [], [], []
    for l in range(depth):
        nw = norm_w[l][None, :]
        qw, kw = _norm_rows(q_norm_w[l]), _norm_rows(k_norm_w[l])
        pscale = pool_scale[l][None, :]

        qt, kb, kt, means, vtf, vtb, sga, sig, yb, plast = _proj_prompt(
            kv_pages if l == depth - 1 else [], xp, nw, w_in_b[l], ones_bd, cos_p, sin_p, qw, kw,
            w_grp_b[l], pscale, w_pb_b[l])
        kv_pages += [kt, vtf]
        pp_l.append(plast)
        q_s, k_s, v_s, kst, vst, sga_s, sig_s, yb_s, pool_s = _proj_sample(
            xs, nw, w_in_b[l], ones_bd, cos_s, sin_s, qw, kw, w_grp_b[l], pscale, w_pb_b[l],
            state_t[l], past_len)

        attn, attn_s = _attn(l, page_table, qt, kb, vtb, means.reshape(bp, -1, ATTN_WIDTH),
                             q_s, k_s, v_s, ck_t, cv_t)

        flat = lambda a: a.reshape(bp * sp, a.shape[-1])
        xp = _merge(flat(xp), flat(attn), flat(sga), flat(sig), flat(yb), w_pa_b[l], w_o_b[l],
                    MERGE_TILE).reshape(bp, sp, d_model)
        xs = _merge(xs, attn_s, sga_s, sig_s, yb_s, w_pa_b[l], w_o_b[l], bs * ss)
        to_steps = lambda a: a.reshape(ss, N_HEADS, HEAD_DIM, bs).transpose(3, 0, 1, 2)
        ks_l.append(to_steps(kst))
        vs_l.append(to_steps(vst))
        ps_l.append(pool_s.transpose(1, 0, 2))

    to_pages = lambda a: a.reshape(depth, bp, sp // page, N_HEADS, HEAD_DIM, page).transpose(0, 1, 2, 5, 3, 4)
    return (xp, xs.reshape(bs, ss, d_model), to_pages(kv_pages[-2]), to_pages(kv_pages[-1]), jnp.stack(pp_l),
            jnp.stack(ks_l), jnp.stack(vs_l), jnp.stack(ps_l))
```

```python
import functools
import math

import jax
import jax.numpy as jnp
from jax import lax
from jax.experimental import pallas as pl
from jax.experimental.pallas import tpu as pltpu

F32 = jnp.float32
BF16 = jnp.bfloat16

N_HEADS = 8
HEAD_DIM = 64
ATTN_WIDTH = N_HEADS * HEAD_DIM
MOBA_BLOCK = 256
MOBA_TOPK = 3
POOL_WINDOWS = (2, 4, 8, 16)
POOL_GROUP_WIDTH = 128
POOL_WIDTH = len(POOL_WINDOWS) * POOL_GROUP_WIDTH
POOL_BUF = max(POOL_WINDOWS) - 1
POOL_HALO = 16
ROPE_THETA = 10000.0
RMS_EPS = 1e-6
NEG_INF = -1e30
ATTN_SCALE = HEAD_DIM ** -0.5
LOG2_E = math.log2(math.e)
BF16_SUBLANES = 16
ACC_ROWS = HEAD_DIM + BF16_SUBLANES

LANES = 128
SUBLANES = 8
VMEM_LIMIT_BYTES = 56 * 1024 * 1024

PROJ_TILE = 2 * MOBA_BLOCK
MERGE_TILE = 4 * MOBA_BLOCK
KV_UNROLL = 4


def _resident(shape):
    return pl.BlockSpec(shape, lambda *_: (0,) * len(shape), pipeline_mode=pl.Buffered(1))


def _params(n_axes):
    return pltpu.CompilerParams(dimension_semantics=("arbitrary",) * n_axes,
                                vmem_limit_bytes=VMEM_LIMIT_BYTES)


def _silu(z):
    return z * jax.nn.sigmoid(z)


def _rms_norm_rows(x, w_row):
    ms = jnp.mean(x * x, axis=-1, keepdims=True)
    return x * lax.rsqrt(ms + RMS_EPS) * w_row


def _head_norm_rope(z, ones_bd, cos_t, sin_t, w_rows):
    m = z.shape[0]
    ssq = jnp.dot((z * z).astype(BF16), ones_bd, preferred_element_type=F32)
    r = lax.rsqrt(ssq * (1.0 / HEAD_DIM) + RMS_EPS)
    cw = cos_t * w_rows[0:1, :]
    sw = sin_t * w_rows[1:2, :]
    lane = lax.broadcasted_iota(jnp.int32, (m, LANES), 1)
    first_half = (lane % HEAD_DIM) < (HEAD_DIM // 2)
    outs = []
    for c in range(ATTN_WIDTH // LANES):
        zc = z[:, c * LANES:(c + 1) * LANES]
        partner = jnp.where(first_half,
                            pltpu.roll(zc, LANES - HEAD_DIM // 2, 1),
                            pltpu.roll(zc, HEAD_DIM // 2, 1))
        outs.append((zc * cw + partner * sw) * r[:, c * LANES:(c + 1) * LANES])
    return jnp.concatenate(outs, axis=-1)


def _col_ranges():
    sizes = (ATTN_WIDTH,) * 4 + (POOL_WIDTH,) * 2
    offs = [0]
    for s in sizes:
        offs.append(offs[-1] + s)
    return offs


def _pool_project(d, zgp, zgb, wgrp_ref, pscale_ref, wpb_ref):
    parts = []
    for g in range(len(POOL_WINDOWS)):
        dg = d[:, g * POOL_GROUP_WIDTH:(g + 1) * POOL_GROUP_WIDTH].astype(BF16)
        parts.append(jnp.dot(dg, wgrp_ref[g], preferred_element_type=F32))
    pool = jnp.concatenate(parts, axis=-1) * pscale_ref[...]
    pg = (pool * _silu(zgp)).astype(BF16)
    b = jnp.dot(pg, wpb_ref[...], preferred_element_type=F32)
    return jax.nn.sigmoid(zgb) * b


N_PROJ_PROMPT_OUT = 10


def _proj_prompt_kernel(x_ref, nw_ref, win_ref, ones_ref, cos_ref, sin_ref, qw_ref, kw_ref,
                        wgrp_ref, pscale_ref, wpb_ref, *rest):
    (qt_out, kb_out, kt_out, mean_out, vtf_out, vtb_out,
     sga_out, sig_out, yb_out, plast_out, ubuf) = rest[-(N_PROJ_PROMPT_OUT + 1):]
    earlier = rest[:-(N_PROJ_PROMPT_OUT + 1)]
    own = len(earlier) // 2
    for slot in range(own):
        kt_out[slot] = earlier[2 * slot][0]
        vtf_out[slot] = earlier[2 * slot + 1][0]
    t = pl.program_id(1)
    tm = x_ref.shape[1]
    d_model = x_ref.shape[2]
    offs = _col_ranges()
    h = _rms_norm_rows(x_ref[0], nw_ref[...]).astype(BF16)

    def proj(lo, hi):
        return jnp.dot(h, win_ref[:, lo:hi], preferred_element_type=F32)

    ones_bd = ones_ref[...]
    cos_t = cos_ref[...]
    sin_t = sin_ref[...]
    n_blk = tm // MOBA_BLOCK
    n_page = tm // LANES

    q = _head_norm_rope(proj(offs[0], offs[1]), ones_bd, cos_t, sin_t, qw_ref[...]) * (ATTN_SCALE * LOG2_E)
    qt = q.T.astype(BF16)
    for i in range(n_blk):
        qt_out[0, i] = qt[:, i * MOBA_BLOCK:(i + 1) * MOBA_BLOCK]

    k = _head_norm_rope(proj(offs[1], offs[2]), ones_bd, cos_t, sin_t, kw_ref[...])
    kb_out[0] = k.astype(BF16)
    kt = k.T
    for i in range(n_page):
        kt_out[own, 0, i] = kt[:, i * LANES:(i + 1) * LANES]
    for i in range(n_blk):
        mean_out[0, i] = jnp.mean(k[i * MOBA_BLOCK:(i + 1) * MOBA_BLOCK], axis=0, keepdims=True)

    vt = proj(offs[2], offs[3]).T
    for i in range(n_page):
        vtf_out[own, 0, i] = vt[:, i * LANES:(i + 1) * LANES]
    vtb = vt.astype(BF16)
    for i in range(n_blk):
        vtb_out[0, i] = vtb[:, i * MOBA_BLOCK:(i + 1) * MOBA_BLOCK]

    sga_out[0] = _silu(proj(offs[3], offs[4])).astype(BF16)

    zu = proj(offs[4], offs[5])

    @pl.when(t == 0)
    def _():
        ubuf[0:POOL_HALO, :] = jnp.zeros((POOL_HALO, POOL_WIDTH), F32)

    ubuf[POOL_HALO:POOL_HALO + tm, :] = zu
    pos = t * tm + lax.broadcasted_iota(jnp.int32, (tm, 1), 0)
    parts = []
    for g, w in enumerate(POOL_WINDOWS):
        cols = slice(g * POOL_GROUP_WIDTH, (g + 1) * POOL_GROUP_WIDTH)
        zug = zu[:, cols]
        acc = zug
        for back in range(1, w):
            acc = acc + ubuf[pl.ds(POOL_HALO - back, tm), cols]
        cnt = jnp.minimum(w, pos + 1).astype(F32)
        parts.append(acc / cnt - zug)
    d = jnp.concatenate(parts, axis=-1)
    ubuf[0:POOL_HALO, :] = zu[tm - POOL_HALO:tm, :]

    @pl.when(t == pl.num_programs(1) - 1)
    def _():
        plast_out[0] = zu[tm - POOL_BUF:tm, :]

    zgp = proj(offs[5], offs[6])
    zgb = proj(offs[6] + d_model, offs[6] + 2 * d_model)
    yb_out[0] = _pool_project(d, zgp, zgb, wgrp_ref, pscale_ref, wpb_ref).astype(BF16)
    sig_out[0] = jax.nn.sigmoid(proj(offs[6], offs[6] + d_model)).astype(BF16)


def _proj_prompt(earlier_pages, x, nw, win, ones_bd, cos_t, sin_t, qw, kw, wgrp, pscale, wpb):
    bsz, seq, d_model = x.shape
    tm = PROJ_TILE
    assert seq % tm == 0 and tm % MOBA_BLOCK == 0
    n_t = seq // tm
    n_blk, n_page = seq // MOBA_BLOCK, seq // LANES
    n_slots = len(earlier_pages) // 2 + 1
    tile = lambda width: pl.BlockSpec((1, tm, width), lambda b, t: (b, t, 0))
    paged = lambda per, minor: pl.BlockSpec((1, per, ATTN_WIDTH, minor), lambda b, t: (b, t, 0, 0))
    pages = lambda slots: pl.BlockSpec((slots, 1, tm // LANES, ATTN_WIDTH, LANES), lambda b, t: (0, b, t, 0, 0))
    out_shape = (
        jax.ShapeDtypeStruct((bsz, n_blk, ATTN_WIDTH, MOBA_BLOCK), BF16),
        jax.ShapeDtypeStruct((bsz, seq, ATTN_WIDTH), BF16),
        jax.ShapeDtypeStruct((n_slots, bsz, n_page, ATTN_WIDTH, LANES), F32),
        jax.ShapeDtypeStruct((bsz, n_blk, 1, ATTN_WIDTH), F32),
        jax.ShapeDtypeStruct((n_slots, bsz, n_page, ATTN_WIDTH, LANES), F32),
        jax.ShapeDtypeStruct((bsz, n_blk, ATTN_WIDTH, MOBA_BLOCK), BF16),
        jax.ShapeDtypeStruct((bsz, seq, ATTN_WIDTH), BF16),
        jax.ShapeDtypeStruct((bsz, seq, d_model), BF16),
        jax.ShapeDtypeStruct((bsz, seq, d_model), BF16),
        jax.ShapeDtypeStruct((bsz, POOL_BUF, POOL_WIDTH), F32),
    )
    out_specs = (
        paged(tm // MOBA_BLOCK, MOBA_BLOCK),
        tile(ATTN_WIDTH),
        pages(n_slots),
        pl.BlockSpec((1, tm // MOBA_BLOCK, 1, ATTN_WIDTH), lambda b, t: (b, t, 0, 0)),
        pages(n_slots),
        paged(tm // MOBA_BLOCK, MOBA_BLOCK),
        tile(ATTN_WIDTH),
        tile(d_model),
        tile(d_model),
        pl.BlockSpec((1, POOL_BUF, POOL_WIDTH), lambda b, t: (b, 0, 0)),
    )
    in_specs = [
        tile(d_model),
        _resident(nw.shape), _resident(win.shape), _resident(ones_bd.shape),
        pl.BlockSpec((tm, LANES), lambda b, t: (t, 0)),
        pl.BlockSpec((tm, LANES), lambda b, t: (t, 0)),
        _resident(qw.shape), _resident(kw.shape),
        _resident(wgrp.shape), _resident(pscale.shape), _resident(wpb.shape),
    ]
    in_specs += [pages(1)] * len(earlier_pages)
    args = [x, nw, win, ones_bd, cos_t, sin_t, qw, kw, wgrp, pscale, wpb, *earlier_pages]
    return pl.pallas_call(
        _proj_prompt_kernel,
        grid=(bsz, n_t),
        in_specs=in_specs,
        out_specs=out_specs,
        out_shape=out_shape,
        scratch_shapes=[pltpu.VMEM((POOL_HALO + tm, POOL_WIDTH), F32)],
        compiler_params=_params(2),
        name="proj_prompt",
    )(*args)


def _select_bias(sc, n_valid, n_blk):
    jrow = lax.broadcasted_iota(jnp.int32, sc.shape, 0)
    jrow_f = jrow.astype(F32)
    valid = jrow < n_valid
    s = jnp.where(valid, sc, -jnp.inf)
    for _ in range(MOBA_TOPK):
        top = jnp.max(s, axis=0, keepdims=True)
        first = jnp.min(jnp.where(s == top, jrow_f, float(n_blk)), axis=0, keepdims=True)
        s = jnp.where(jrow_f == first, -jnp.inf, s)
    return jnp.where(valid & (s == -jnp.inf), 0.0, NEG_INF)


def _attn_prompt_tile(i, qt_ref, kb_ref, vtb_ref, mean_ref, o_ref, qaug_scr, sa_scr, sb_scr, m_scr, acc_scr,
                      before_loops):
    n_blk = mean_ref.shape[1]
    blk = MOBA_BLOCK
    pair_w = 2 * HEAD_DIM
    qt = qt_ref[0, 0]

    means = mean_ref[0].astype(BF16)
    tiled = jnp.concatenate([means] * N_HEADS, axis=0)
    row_head = lax.broadcasted_iota(jnp.int32, tiled.shape, 0) // n_blk
    col_head = lax.broadcasted_iota(jnp.int32, tiled.shape, 1) // HEAD_DIM
    means_bd = jnp.where(row_head == col_head, tiled, jnp.zeros_like(tiled))
    sc = jnp.dot(means_bd, qt, preferred_element_type=F32)

    pair_row = lax.broadcasted_iota(jnp.int32, (pair_w, blk), 0)
    blk_row = lax.broadcasted_iota(jnp.int32, (n_blk, blk), 0)
    tail_row = lax.broadcasted_iota(jnp.int32, (pair_w - n_blk, blk), 0)
    tail = jnp.where(tail_row == 0, NEG_INF, 0.0)
    for h in range(N_HEADS):
        bias = _select_bias(sc[h * n_blk:(h + 1) * n_blk], i, n_blk)
        bias = jnp.where(blk_row == i, 0.0, bias)
        qpair = qt[(h // 2) * pair_w:(h // 2 + 1) * pair_w, :]
        mine = (pair_row < HEAD_DIM) if h % 2 == 0 else (pair_row >= HEAD_DIM)
        qh = jnp.where(mine, qpair, jnp.zeros_like(qpair))
        qaug_scr[h] = jnp.concatenate([qh, jnp.concatenate([bias, tail], axis=0).astype(BF16)], axis=0)

    lane_blk = lax.broadcasted_iota(jnp.int32, (blk, pair_w), 1)

    def keys_aug(j, bias_row):
        start = pl.multiple_of(j * blk, blk)
        onehot = jnp.where(lane_blk == bias_row, 1.0, 0.0).astype(BF16)
        return [jnp.concatenate([kb_ref[0, pl.ds(start, blk), p * pair_w:(p + 1) * pair_w], onehot], axis=1)
                for p in range(N_HEADS // 2)]

    def scores(kj, h):
        return jnp.dot(kj[h // 2], qaug_scr[h], preferred_element_type=F32)

    ones_rows = jnp.ones((ACC_ROWS - HEAD_DIM, blk), BF16)

    def pv_and_sum(h, v_blk, p):
        vt_h = vtb_ref[0, v_blk, h * HEAD_DIM:(h + 1) * HEAD_DIM, :]
        return jnp.dot(jnp.concatenate([vt_h, ones_rows], axis=0), p.astype(BF16), preferred_element_type=F32)

    def first_block(h, st, v_blk):
        m0 = jnp.max(st, axis=0, keepdims=True)
        m_scr[h] = m0
        acc_scr[h] = pv_and_sum(h, v_blk, jnp.exp2(st - m0))

    def next_block(h, sj, v_blk):
        m_old = m_scr[h]
        m_new = jnp.maximum(m_old, jnp.max(sj, axis=0, keepdims=True))
        acc_scr[h] = jnp.exp2(m_old - m_new) * acc_scr[h] + pv_and_sum(h, v_blk, jnp.exp2(sj - m_new))
        m_scr[h] = m_new

    key_i = lax.broadcasted_iota(jnp.int32, (blk, blk), 0)
    qry_i = lax.broadcasted_iota(jnp.int32, (blk, blk), 1)
    causal = key_i <= qry_i
    ka = keys_aug(i, i)
    for h in range(N_HEADS):
        sb_scr[h] = scores(ka, h)
    k0 = keys_aug(0, 0)
    for h in range(N_HEADS):
        sa_scr[h] = scores(k0, h)
        first_block(h, jnp.where(causal, sb_scr[h], NEG_INF), i)

    def make_body(unroll):
        def body(t, base):
            for u in range(unroll):
                cur = base + t * unroll + u
                nxt = cur + 1
                nxt_keys = jnp.minimum(nxt, i - 1)
                k_aug = keys_aug(nxt_keys, jnp.where(nxt < i, nxt, n_blk))
                read, write = (sa_scr, sb_scr) if u % 2 == 0 else (sb_scr, sa_scr)
                for h in range(N_HEADS):
                    write[h] = scores(k_aug, h)
                    next_block(h, read[h], jnp.minimum(cur, i - 1))
            return base
        return body

    before_loops()
    n_wide = i // KV_UNROLL
    lax.fori_loop(0, n_wide, make_body(KV_UNROLL), 0)
    done = n_wide * KV_UNROLL
    lax.fori_loop(0, (i - done + 1) // 2, make_body(2), done)

    outs = []
    for h in range(N_HEADS):
        acc = acc_scr[h]
        outs.append(acc[0:HEAD_DIM] / acc[HEAD_DIM:HEAD_DIM + 1])
    o_ref[0] = jnp.concatenate(outs, axis=0).T.astype(BF16)


def _attn_sample_seq(q, k_new, v_new, kt, vt):
    n_new = q.shape[0]
    n_full = kt.shape[1] // MOBA_BLOCK
    tiled = jnp.concatenate([q] * N_HEADS, axis=0)
    row_head = lax.broadcasted_iota(jnp.int32, tiled.shape, 0) // n_new
    col_head = lax.broadcasted_iota(jnp.int32, tiled.shape, 1) // HEAD_DIM
    head_lanes = row_head == col_head
    q_bd = jnp.where(head_lanes, tiled, 0.0).astype(BF16)

    s_past = jnp.dot(q_bd, kt, preferred_element_type=F32)
    nt_dims = (((1,), (1,)), ((), ()))
    s_new = lax.dot_general(q_bd, k_new.astype(BF16), nt_dims, preferred_element_type=F32)
    row_step = lax.broadcasted_iota(jnp.int32, s_new.shape, 0) % n_new
    col_step = lax.broadcasted_iota(jnp.int32, s_new.shape, 1)
    s_new = jnp.where(col_step <= row_step, s_new, NEG_INF)

    blocks = [s_past[:, n * MOBA_BLOCK:(n + 1) * MOBA_BLOCK] for n in range(n_full)]
    score = [jnp.sum(sb, axis=1, keepdims=True) for sb in blocks]
    n_sel = min(MOBA_TOPK, n_full)
    lane = lax.broadcasted_iota(jnp.int32, (tiled.shape[0], LANES), 1)
    by_lane = jnp.full(lane.shape, -jnp.inf, F32)
    for n in range(n_full):
        by_lane = jnp.where(lane == n, score[n], by_lane)
    masked = []
    for n in range(n_full):
        ahead = (by_lane > score[n]) | ((lane < n) & (by_lane == score[n]))
        rank = jnp.sum(jnp.where(ahead, 1.0, 0.0), axis=1, keepdims=True)
        masked.append(blocks[n] + jnp.where(rank < n_sel, 0.0, NEG_INF))

    m = jnp.max(s_new, axis=1, keepdims=True)
    for sb in masked:
        m = jnp.maximum(m, jnp.max(sb, axis=1, keepdims=True))
    p_new = jnp.exp(s_new - m)
    l = jnp.sum(p_new, axis=1, keepdims=True)
    probs = []
    for sb in masked:
        pb = jnp.exp(sb - m)
        l = l + jnp.sum(pb, axis=1, keepdims=True)
        probs.append(pb.astype(BF16))
    p_past = jnp.concatenate(probs, axis=1)
    out = lax.dot_general(p_past, vt, nt_dims, preferred_element_type=F32)
    out = out + jnp.dot(p_new.astype(BF16), v_new.astype(BF16), preferred_element_type=F32)
    out = jnp.where(head_lanes, out / l, 0.0)
    res = out[0:n_new]
    for hh in range(1, N_HEADS):
        res = res + out[hh * n_new:(hh + 1) * n_new]
    return res


def _attn_kernel(layer, pt_ref, qt_ref, kb_ref, vtb_ref, mean_ref, qs_ref, ks_ref, vs_ref, ck_hbm, cv_hbm,
                 o_ref, os_ref, qaug_scr, sa_scr, sb_scr, m_scr, acc_scr, kbuf, vbuf, sem):
    step = pl.program_id(0) * pl.num_programs(1) + pl.program_id(1)
    n_slots = kbuf.shape[0]
    per_step = n_slots - 1
    n_seq = pt_ref.shape[0]
    n_pages = pt_ref.shape[1]
    page = ck_hbm.shape[3]
    n_new = qs_ref.shape[0] // per_step
    first = step * per_step

    def page_copies(seq):
        slot = lax.rem(seq, n_slots)
        copies = []
        for pg in range(n_pages):
            phys = pt_ref[seq, pg]
            win = pl.ds(pg * page, page)
            copies.append(pltpu.make_async_copy(ck_hbm.at[layer, phys], kbuf.at[slot, :, win], sem.at[0, slot]))
            copies.append(pltpu.make_async_copy(cv_hbm.at[layer, phys], vbuf.at[slot, :, win], sem.at[1, slot]))
        return copies

    def start_pages(seq):
        @pl.when(seq < n_seq)
        def _():
            for c in page_copies(seq):
                c.start()

    @pl.when(step == 0)
    def _():
        for j in range(per_step):
            for c in page_copies(j):
                c.start()

    start_pages(first + per_step)
    for j in range(per_step):
        for c in page_copies(first + j):
            c.wait()
    for j in range(per_step):
        rows = slice(j * n_new, (j + 1) * n_new)
        slot = lax.rem(first + j, n_slots)
        os_ref[rows, :] = _attn_sample_seq(qs_ref[rows, :], ks_ref[rows, :], vs_ref[rows, :],
                                           kbuf[slot].astype(BF16), vbuf[slot].astype(BF16))

    def start_rest_of_next_step():
        for j in range(1, per_step):
            start_pages(first + per_step + j)

    _attn_prompt_tile(pl.program_id(1), qt_ref, kb_ref, vtb_ref, mean_ref, o_ref,
                      qaug_scr, sa_scr, sb_scr, m_scr, acc_scr, start_rest_of_next_step)


def _attn(layer, page_table, qt, kb, vtb, means, q_s, k_s, v_s, ck_t, cv_t):
    bsz, n_blk, width, blk = qt.shape
    seq = kb.shape[1]
    assert n_blk < 2 * HEAD_DIM
    n_seq, n_pages = page_table.shape
    n_tok = q_s.shape[0]
    n_new = n_tok // n_seq
    page = ck_t.shape[3]
    past = n_pages * page
    assert past % MOBA_BLOCK == 0 and past >= MOBA_BLOCK and ck_t.shape[2] == width
    n_steps = bsz * n_blk
    assert n_seq % n_steps == 0
    per_step = n_seq // n_steps
    sample_tile = pl.BlockSpec((per_step * n_new, width), lambda b, i, pt: (b * n_blk + i, 0))
    grid_spec = pltpu.PrefetchScalarGridSpec(
        num_scalar_prefetch=1,
        grid=(bsz, n_blk),
        in_specs=[
            pl.BlockSpec((1, 1, width, blk), lambda b, i, pt: (b, i, 0, 0)),
            pl.BlockSpec((1, seq, width), lambda b, i, pt: (b, 0, 0), pipeline_mode=pl.Buffered(1)),
            pl.BlockSpec((1, n_blk, width, blk), lambda b, i, pt: (b, 0, 0, 0), pipeline_mode=pl.Buffered(1)),
            pl.BlockSpec((1, n_blk, width), lambda b, i, pt: (b, 0, 0)),
            sample_tile, sample_tile, sample_tile,
            pl.BlockSpec(memory_space=pl.ANY), pl.BlockSpec(memory_space=pl.ANY),
        ],
        out_specs=(pl.BlockSpec((1, blk, width), lambda b, i, pt: (b, i, 0)), sample_tile),
        scratch_shapes=[pltpu.VMEM((N_HEADS, 4 * HEAD_DIM, blk), BF16),
                        pltpu.VMEM((N_HEADS, blk, blk), F32),
                        pltpu.VMEM((N_HEADS, blk, blk), F32),
                        pltpu.VMEM((N_HEADS, 1, blk), F32),
                        pltpu.VMEM((N_HEADS, ACC_ROWS, blk), F32),
                        pltpu.VMEM((per_step + 1, width, past), F32),
                        pltpu.VMEM((per_step + 1, width, past), F32),
                        pltpu.SemaphoreType.DMA((2, per_step + 1))],
    )
    return pl.pallas_call(
        functools.partial(_attn_kernel, layer),
        grid_spec=grid_spec,
        out_shape=(jax.ShapeDtypeStruct((bsz, seq, width), BF16), jax.ShapeDtypeStruct((n_tok, width), F32)),
        compiler_params=_params(2),
        name="attn",
    )(page_table, qt, kb, vtb, means, q_s, k_s, v_s, ck_t, cv_t)


def _merge_kernel(x_ref, attn_ref, sga_ref, sig_ref, yb_ref, wpa_ref, wo_ref, o_ref):
    gated = attn_ref[...].astype(BF16) * sga_ref[...]
    a = jnp.dot(gated, wpa_ref[...], preferred_element_type=F32)
    y = sig_ref[...].astype(F32) * a + yb_ref[...].astype(F32)
    o_ref[...] = x_ref[...] + jnp.dot(y.astype(BF16), wo_ref[...], preferred_element_type=F32)


def _merge(x, attn, sga, sig, yb, wpa, wo, tm):
    n, d_model = x.shape
    assert n % tm == 0
    tile = lambda width: pl.BlockSpec((tm, width), lambda t: (t, 0))
    return pl.pallas_call(
        _merge_kernel,
        grid=(n // tm,),
        in_specs=[tile(d_model), tile(ATTN_WIDTH), tile(ATTN_WIDTH), tile(d_model), tile(d_model),
                  _resident(wpa.shape), _resident(wo.shape)],
        out_specs=tile(d_model),
        out_shape=jax.ShapeDtypeStruct((n, d_model), F32),
        compiler_params=_params(1),
        name="merge",
    )(x, attn, sga, sig, yb, wpa, wo)


def _proj_sample_kernel(past_len, x_ref, nw_ref, win_ref, ones_ref, cos_ref, sin_ref, qw_ref, kw_ref,
                        wgrp_ref, pscale_ref, wpb_ref, state_ref,
                        q_out, k_out, v_out, kst_out, vst_out, sga_out, sig_out, yb_out, pool_out,
                        slab_scr):
    n_tok, d_model = x_ref.shape
    n_seq = state_ref.shape[1]
    n_new = n_tok // n_seq
    n_slab = slab_scr.shape[0]
    offs = _col_ranges()
    h = _rms_norm_rows(x_ref[...], nw_ref[...]).astype(BF16)

    def proj(lo, hi):
        return jnp.dot(h, win_ref[:, lo:hi], preferred_element_type=F32)

    def to_slabs(val):
        for c in range(n_slab):
            slab_scr[c] = val[:, c * LANES:(c + 1) * LANES]

    def step_rows(s):
        return jnp.concatenate(
            [slab_scr[c, pl.ds(s, n_seq, stride=n_new), :] for c in range(n_slab)], axis=-1)

    ones_bd = ones_ref[...]
    cos_t = cos_ref[...]
    sin_t = sin_ref[...]
    q_out[...] = _head_norm_rope(proj(offs[0], offs[1]), ones_bd, cos_t, sin_t, qw_ref[...]) * ATTN_SCALE

    k = _head_norm_rope(proj(offs[1], offs[2]), ones_bd, cos_t, sin_t, kw_ref[...])
    k_out[...] = k
    to_slabs(k)
    for s in range(n_new):
        kst_out[s] = step_rows(s).T

    v = proj(offs[2], offs[3])
    v_out[...] = v
    to_slabs(v)
    for s in range(n_new):
        vst_out[s] = step_rows(s).T

    sga_out[...] = _silu(proj(offs[3], offs[4])).astype(BF16)

    zu = proj(offs[4], offs[5])
    to_slabs(zu)
    hist = [state_ref[j] for j in range(POOL_BUF)] + [step_rows(s) for s in range(n_new)]
    for j in range(POOL_BUF):
        pool_out[j] = hist[len(hist) - POOL_BUF + j]
    for s in range(n_new):
        cur = POOL_BUF + s
        parts = []
        for g, w in enumerate(POOL_WINDOWS):
            cols = slice(g * POOL_GROUP_WIDTH, (g + 1) * POOL_GROUP_WIDTH)
            acc = hist[cur][:, cols]
            for back in range(1, w):
                acc = acc + hist[cur - back][:, cols]
            cnt = float(min(w, past_len + s + 1))
            parts.append(acc / cnt - hist[cur][:, cols])
        ds = jnp.concatenate(parts, axis=-1)
        for c in range(n_slab):
            slab_scr[c, pl.ds(s, n_seq, stride=n_new), :] = ds[:, c * LANES:(c + 1) * LANES]
    d = jnp.concatenate([slab_scr[c] for c in range(n_slab)], axis=-1)

    zgp = proj(offs[5], offs[6])
    zgb = proj(offs[6] + d_model, offs[6] + 2 * d_model)
    yb_out[...] = _pool_project(d, zgp, zgb, wgrp_ref, pscale_ref, wpb_ref).astype(BF16)
    sig_out[...] = jax.nn.sigmoid(proj(offs[6], offs[6] + d_model)).astype(BF16)


def _proj_sample(x, nw, win, ones_bd, cos_t, sin_t, qw, kw, wgrp, pscale, wpb, state_t, past_len):
    n_tok, d_model = x.shape
    n_hist, n_seq, pool_w = state_t.shape
    assert n_hist == POOL_BUF and pool_w == POOL_WIDTH and n_tok % n_seq == 0
    n_new = n_tok // n_seq
    assert ATTN_WIDTH == POOL_WIDTH
    n_slab = POOL_WIDTH // LANES
    full = lambda shape: pl.BlockSpec(shape, lambda t: (0,) * len(shape))
    out_shape = (
        jax.ShapeDtypeStruct((n_tok, ATTN_WIDTH), F32),
        jax.ShapeDtypeStruct((n_tok, ATTN_WIDTH), F32),
        jax.ShapeDtypeStruct((n_tok, ATTN_WIDTH), F32),
        jax.ShapeDtypeStruct((n_new, ATTN_WIDTH, n_seq), F32),
        jax.ShapeDtypeStruct((n_new, ATTN_WIDTH, n_seq), F32),
        jax.ShapeDtypeStruct((n_tok, ATTN_WIDTH), BF16),
        jax.ShapeDtypeStruct((n_tok, d_model), BF16),
        jax.ShapeDtypeStruct((n_tok, d_model), BF16),
        jax.ShapeDtypeStruct((POOL_BUF, n_seq, POOL_WIDTH), F32),
    )
    args = (x, nw, win, ones_bd, cos_t, sin_t, qw, kw, wgrp, pscale, wpb, state_t)
    return pl.pallas_call(
        functools.partial(_proj_sample_kernel, past_len),
        grid=(1,),
        in_specs=[full(a.shape) for a in args],
        out_specs=tuple(full(o.shape) for o in out_shape),
        out_shape=out_shape,
        scratch_shapes=[pltpu.VMEM((n_slab, n_tok, LANES), F32)],
        compiler_params=_params(1),
        name="proj_sample",
    )(*args)


def _rope_tables(pos):
    half = HEAD_DIM // 2
    inv_freq = jnp.exp(-math.log(ROPE_THETA) * jnp.arange(half, dtype=F32) / half)
    ang = pos.astype(F32)[:, None] * inv_freq[None, :]
    cos, sin = jnp.cos(ang), jnp.sin(ang)
    reps = LANES // HEAD_DIM
    return (jnp.tile(jnp.concatenate([cos, cos], axis=-1), (1, reps)),
            jnp.tile(jnp.concatenate([-sin, sin], axis=-1), (1, reps)))


def _norm_rows(w):
    half = HEAD_DIM // 2
    reps = LANES // HEAD_DIM
    swapped = jnp.concatenate([w[half:], w[:half]])
    return jnp.stack([jnp.tile(w, reps), jnp.tile(swapped, reps)]).astype(F32)


def kernel(x_prompt, x_sample, cache_k, cache_v, state_pool, page_table, norm_w, w_in, q_norm_w,
           k_norm_w, w_pool_grp, pool_scale, w_proj_attn, w_proj_pool, w_out):
    bp, sp, d_model = x_prompt.shape
    bs, ss, _ = x_sample.shape
    depth, n_phys, page, n_heads, head_dim = cache_k.shape
    assert (n_heads, head_dim) == (N_HEADS, HEAD_DIM) and page == LANES
    n_pages = page_table.shape[1]
    past_len = n_pages * page

    cos_p, sin_p = _rope_tables(jnp.arange(sp, dtype=jnp.int32))
    cos_s, sin_s = _rope_tables(past_len + jnp.arange(ss, dtype=jnp.int32))
    cos_s, sin_s = jnp.tile(cos_s, (bs, 1)), jnp.tile(sin_s, (bs, 1))
    ones_bd = jnp.kron(jnp.eye(N_HEADS, dtype=F32), jnp.ones((HEAD_DIM, HEAD_DIM), F32)).astype(BF16)

    ck_t = cache_k.transpose(0, 1, 3, 4, 2).reshape(depth, n_phys, ATTN_WIDTH, page)
    cv_t = cache_v.transpose(0, 1, 3, 4, 2).reshape(depth, n_phys, ATTN_WIDTH, page)
    state_t = state_pool.transpose(0, 2, 1, 3)

    w_in_b = w_in.astype(BF16)
    w_grp_b = w_pool_grp.astype(BF16)
    w_pa_b = w_proj_attn.astype(BF16)
    w_pb_b = w_proj_pool.astype(BF16)
    w_o_b = w_out.astype(BF16)

    xp = x_prompt
    xs = x_sample.reshape(bs * ss, d_model)
    kv_pages = []
    pp_l, ks_l, vs_l, ps_l = [], [], [], []
    for l in range(depth):
        nw = norm_w[l][None, :]
        qw, kw = _norm_rows(q_norm_w[l]), _norm_rows(k_norm_w[l])
        pscale = pool_scale[l][None, :]

        qt, kb, kt, means, vtf, vtb, sga, sig, yb, plast = _proj_prompt(
            kv_pages if l == depth - 1 else [], xp, nw, w_in_b[l], ones_bd, cos_p, sin_p, qw, kw,
            w_grp_b[l], pscale, w_pb_b[l])
        kv_pages += [kt, vtf]
        pp_l.append(plast)
        q_s, k_s, v_s, kst, vst, sga_s, sig_s, yb_s, pool_s = _proj_sample(
            xs, nw, w_in_b[l], ones_bd, cos_s, sin_s, qw, kw, w_grp_b[l], pscale, w_pb_b[l],
            state_t[l], past_len)

        attn, attn_s = _attn(l, page_table, qt, kb, vtb, means.reshape(bp, -1, ATTN_WIDTH),
                             q_s, k_s, v_s, ck_t, cv_t)

        flat = lambda a: a.reshape(bp * sp, a.shape[-1])
        xp = _merge(flat(xp), flat(attn), flat(sga), flat(sig), flat(yb), w_pa_b[l], w_o_b[l],
                    MERGE_TILE).reshape(bp, sp, d_model)
        xs = _merge(xs, attn_s, sga_s, sig_s, yb_s, w_pa_b[l], w_o_b[l], bs * ss)
        to_steps = lambda a: a.reshape(ss, N_HEADS, HEAD_DIM, bs).transpose(3, 0, 1, 2)
        ks_l.append(to_steps(kst))
        vs_l.append(to_steps(vst))
        ps_l.append(pool_s.transpose(1, 0, 2))

    to_pages = lambda a: a.reshape(depth, bp, sp // page, N_HEADS, HEAD_DIM, page).transpose(0, 1, 2, 5, 3, 4)
    return (xp, xs.reshape(bs, ss, d_model), to_pages(kv_pages[-2]), to_pages(kv_pages[-1]), jnp.stack(pp_l),
            jnp.stack(ks_l), jnp.stack(vs_l), jnp.stack(ps_l))
```

```python
import functools
import math

import jax
import jax.numpy as jnp
from jax import lax
from jax.experimental import pallas as pl
from jax.experimental.pallas import tpu as pltpu

F32 = jnp.float32
BF16 = jnp.bfloat16

N_HEADS = 8
HEAD_DIM = 64
ATTN_WIDTH = N_HEADS * HEAD_DIM
MOBA_BLOCK = 256
MOBA_TOPK = 3
POOL_WINDOWS = (2, 4, 8, 16)
POOL_GROUP_WIDTH = 128
POOL_WIDTH = len(POOL_WINDOWS) * POOL_GROUP_WIDTH
POOL_BUF = max(POOL_WINDOWS) - 1
POOL_HALO = 16
ROPE_THETA = 10000.0
RMS_EPS = 1e-6
NEG_INF = -1e30
ATTN_SCALE = HEAD_DIM ** -0.5
LOG2_E = math.log2(math.e)
BF16_SUBLANES = 16
ACC_ROWS = HEAD_DIM + BF16_SUBLANES

LANES = 128
SUBLANES = 8
VMEM_LIMIT_BYTES = 56 * 1024 * 1024

PROJ_TILE = 2 * MOBA_BLOCK
MERGE_TILE = 4 * MOBA_BLOCK
KV_UNROLL = 4


def _resident(shape):
    return pl.BlockSpec(shape, lambda *_: (0,) * len(shape), pipeline_mode=pl.Buffered(1))


def _layer_slice(stacked, layer):
    rest = stacked.shape[1:]
    return pl.BlockSpec((None,) + rest, lambda *_: (layer,) + (0,) * len(rest), pipeline_mode=pl.Buffered(1))


def _params(n_axes):
    return pltpu.CompilerParams(dimension_semantics=("arbitrary",) * n_axes,
                                vmem_limit_bytes=VMEM_LIMIT_BYTES)


def _silu(z):
    return z * jax.nn.sigmoid(z)


def _rms_norm_rows(x, w_row):
    ms = jnp.mean(x * x, axis=-1, keepdims=True)
    return x * lax.rsqrt(ms + RMS_EPS) * w_row


def _head_norm_rope(z, ones_bd, cos_t, sin_t, w_rows):
    m = z.shape[0]
    ssq = jnp.dot((z * z).astype(BF16), ones_bd, preferred_element_type=F32)
    r = lax.rsqrt(ssq * (1.0 / HEAD_DIM) + RMS_EPS)
    cw = cos_t * w_rows[0:1, :]
    sw = sin_t * w_rows[1:2, :]
    lane = lax.broadcasted_iota(jnp.int32, (m, LANES), 1)
    first_half = (lane % HEAD_DIM) < (HEAD_DIM // 2)
    outs = []
    for c in range(ATTN_WIDTH // LANES):
        zc = z[:, c * LANES:(c + 1) * LANES]
        partner = jnp.where(first_half,
                            pltpu.roll(zc, LANES - HEAD_DIM // 2, 1),
                            pltpu.roll(zc, HEAD_DIM // 2, 1))
        outs.append((zc * cw + partner * sw) * r[:, c * LANES:(c + 1) * LANES])
    return jnp.concatenate(outs, axis=-1)


def _col_ranges():
    sizes = (ATTN_WIDTH,) * 4 + (POOL_WIDTH,) * 2
    offs = [0]
    for s in sizes:
        offs.append(offs[-1] + s)
    return offs


def _pool_project(d, zgp, zgb, wgrp_ref, pscale_ref, wpb_ref):
    parts = []
    for g in range(len(POOL_WINDOWS)):
        dg = d[:, g * POOL_GROUP_WIDTH:(g + 1) * POOL_GROUP_WIDTH].astype(BF16)
        parts.append(jnp.dot(dg, wgrp_ref[g], preferred_element_type=F32))
    pool = jnp.concatenate(parts, axis=-1) * pscale_ref[...]
    pg = (pool * _silu(zgp)).astype(BF16)
    b = jnp.dot(pg, wpb_ref[...], preferred_element_type=F32)
    return jax.nn.sigmoid(zgb) * b


N_PROJ_PROMPT_OUT = 10


def _proj_prompt_kernel(x_ref, nw_ref, win_ref, ones_ref, cos_ref, sin_ref, qw_ref, kw_ref,
                        wgrp_ref, pscale_ref, wpb_ref, *rest):
    (qt_out, kb_out, kt_out, mean_out, vtf_out, vtb_out,
     sga_out, sig_out, yb_out, plast_out, ubuf) = rest[-(N_PROJ_PROMPT_OUT + 1):]
    earlier = rest[:-(N_PROJ_PROMPT_OUT + 1)]
    own = len(earlier) // 2
    for slot in range(own):
        kt_out[slot] = earlier[2 * slot][0]
        vtf_out[slot] = earlier[2 * slot + 1][0]
    t = pl.program_id(1)
    tm = x_ref.shape[1]
    d_model = x_ref.shape[2]
    offs = _col_ranges()

    @pl.when((pl.program_id(0) == 0) & (t == 0))
    def _():
        ubuf[0:POOL_HALO, :] = jnp.zeros((POOL_HALO, POOL_WIDTH), F32)

    h = _rms_norm_rows(x_ref[0], nw_ref[...]).astype(BF16)

    def proj(lo, hi):
        return jnp.dot(h, win_ref[:, lo:hi], preferred_element_type=F32)

    ones_bd = ones_ref[...]
    cos_t = cos_ref[...]
    sin_t = sin_ref[...]
    n_blk = tm // MOBA_BLOCK
    n_page = tm // LANES

    q = _head_norm_rope(proj(offs[0], offs[1]), ones_bd, cos_t, sin_t, qw_ref[...]) * (ATTN_SCALE * LOG2_E)
    qt = q.T.astype(BF16)
    for i in range(n_blk):
        qt_out[0, i] = qt[:, i * MOBA_BLOCK:(i + 1) * MOBA_BLOCK]

    k = _head_norm_rope(proj(offs[1], offs[2]), ones_bd, cos_t, sin_t, kw_ref[...])
    kb_out[0] = k.astype(BF16)
    kt = k.T
    for i in range(n_page):
        kt_out[own, 0, i] = kt[:, i * LANES:(i + 1) * LANES]
    for i in range(n_blk):
        mean_out[0, i] = jnp.mean(k[i * MOBA_BLOCK:(i + 1) * MOBA_BLOCK], axis=0, keepdims=True)

    vt = proj(offs[2], offs[3]).T
    for i in range(n_page):
        vtf_out[own, 0, i] = vt[:, i * LANES:(i + 1) * LANES]
    vtb = vt.astype(BF16)
    for i in range(n_blk):
        vtb_out[0, i] = vtb[:, i * MOBA_BLOCK:(i + 1) * MOBA_BLOCK]

    sga_out[0] = _silu(proj(offs[3], offs[4])).astype(BF16)

    zu = proj(offs[4], offs[5])
    ubuf[POOL_HALO:POOL_HALO + tm, :] = zu
    pos = t * tm + lax.broadcasted_iota(jnp.int32, (tm, 1), 0)
    parts = []
    for g, w in enumerate(POOL_WINDOWS):
        cols = slice(g * POOL_GROUP_WIDTH, (g + 1) * POOL_GROUP_WIDTH)
        zug = zu[:, cols]
        acc = zug
        for back in range(1, w):
            acc = acc + ubuf[pl.ds(POOL_HALO - back, tm), cols]
        cnt = jnp.minimum(w, pos + 1).astype(F32)
        parts.append(acc / cnt - zug)
    d = jnp.concatenate(parts, axis=-1)
    last = t == pl.num_programs(1) - 1
    ubuf[0:POOL_HALO, :] = jnp.where(last, 0.0, zu[tm - POOL_HALO:tm, :])
    plast_out[0] = zu[tm - POOL_BUF:tm, :]

    zgp = proj(offs[5], offs[6])
    zgb = proj(offs[6] + d_model, offs[6] + 2 * d_model)
    yb_out[0] = _pool_project(d, zgp, zgb, wgrp_ref, pscale_ref, wpb_ref).astype(BF16)
    sig_out[0] = jax.nn.sigmoid(proj(offs[6], offs[6] + d_model)).astype(BF16)


def _proj_prompt(layer, earlier_pages, x, nw, win, ones_bd, cos_t, sin_t, qw, kw, wgrp, pscale, wpb):
    bsz, seq, d_model = x.shape
    tm = PROJ_TILE
    assert seq % tm == 0 and tm % MOBA_BLOCK == 0
    n_t = seq // tm
    n_blk, n_page = seq // MOBA_BLOCK, seq // LANES
    n_slots = len(earlier_pages) // 2 + 1
    tile = lambda width: pl.BlockSpec((1, tm, width), lambda b, t: (b, t, 0))
    paged = lambda per, minor: pl.BlockSpec((1, per, ATTN_WIDTH, minor), lambda b, t: (b, t, 0, 0))
    pages = lambda slots: pl.BlockSpec((slots, 1, tm // LANES, ATTN_WIDTH, LANES), lambda b, t: (0, b, t, 0, 0))
    out_shape = (
        jax.ShapeDtypeStruct((bsz, n_blk, ATTN_WIDTH, MOBA_BLOCK), BF16),
        jax.ShapeDtypeStruct((bsz, seq, ATTN_WIDTH), BF16),
        jax.ShapeDtypeStruct((n_slots, bsz, n_page, ATTN_WIDTH, LANES), F32),
        jax.ShapeDtypeStruct((bsz, n_blk, 1, ATTN_WIDTH), F32),
        jax.ShapeDtypeStruct((n_slots, bsz, n_page, ATTN_WIDTH, LANES), F32),
        jax.ShapeDtypeStruct((bsz, n_blk, ATTN_WIDTH, MOBA_BLOCK), BF16),
        jax.ShapeDtypeStruct((bsz, seq, ATTN_WIDTH), BF16),
        jax.ShapeDtypeStruct((bsz, seq, d_model), BF16),
        jax.ShapeDtypeStruct((bsz, seq, d_model), BF16),
        jax.ShapeDtypeStruct((bsz, POOL_BUF, POOL_WIDTH), F32),
    )
    out_specs = (
        paged(tm // MOBA_BLOCK, MOBA_BLOCK),
        tile(ATTN_WIDTH),
        pages(n_slots),
        pl.BlockSpec((1, tm // MOBA_BLOCK, 1, ATTN_WIDTH), lambda b, t: (b, t, 0, 0)),
        pages(n_slots),
        paged(tm // MOBA_BLOCK, MOBA_BLOCK),
        tile(ATTN_WIDTH),
        tile(d_model),
        tile(d_model),
        pl.BlockSpec((1, POOL_BUF, POOL_WIDTH), lambda b, t: (b, 0, 0)),
    )
    in_specs = [
        tile(d_model),
        _resident(nw.shape), _layer_slice(win, layer), _resident(ones_bd.shape),
        pl.BlockSpec((tm, LANES), lambda b, t: (t, 0)),
        pl.BlockSpec((tm, LANES), lambda b, t: (t, 0)),
        _resident(qw.shape), _resident(kw.shape),
        _layer_slice(wgrp, layer), _resident(pscale.shape), _layer_slice(wpb, layer),
    ]
    in_specs += [pages(1)] * len(earlier_pages)
    args = [x, nw, win, ones_bd, cos_t, sin_t, qw, kw, wgrp, pscale, wpb, *earlier_pages]
    return pl.pallas_call(
        _proj_prompt_kernel,
        grid=(bsz, n_t),
        in_specs=in_specs,
        out_specs=out_specs,
        out_shape=out_shape,
        scratch_shapes=[pltpu.VMEM((POOL_HALO + tm, POOL_WIDTH), F32)],
        compiler_params=_params(2),
        name="proj_prompt",
    )(*args)


def _select_bias(sc, n_valid, n_blk):
    jrow = lax.broadcasted_iota(jnp.int32, sc.shape, 0)
    jrow_f = jrow.astype(F32)
    valid = jrow < n_valid
    s = jnp.where(valid, sc, -jnp.inf)
    for _ in range(MOBA_TOPK):
        top = jnp.max(s, axis=0, keepdims=True)
        first = jnp.min(jnp.where(s == top, jrow_f, float(n_blk)), axis=0, keepdims=True)
        s = jnp.where(jrow_f == first, -jnp.inf, s)
    return jnp.where(valid & (s == -jnp.inf), 0.0, NEG_INF)


def _attn_prompt_tile(i, qt_ref, kb_ref, vtb_ref, mean_ref, o_ref, qaug_scr, sa_scr, sb_scr, m_scr, acc_scr,
                      before_loops):
    n_blk = mean_ref.shape[1]
    blk = MOBA_BLOCK
    pair_w = 2 * HEAD_DIM
    qt = qt_ref[0, 0]

    means = mean_ref[0].astype(BF16)
    tiled = jnp.concatenate([means] * N_HEADS, axis=0)
    row_head = lax.broadcasted_iota(jnp.int32, tiled.shape, 0) // n_blk
    col_head = lax.broadcasted_iota(jnp.int32, tiled.shape, 1) // HEAD_DIM
    means_bd = jnp.where(row_head == col_head, tiled, jnp.zeros_like(tiled))
    sc = jnp.dot(means_bd, qt, preferred_element_type=F32)

    pair_row = lax.broadcasted_iota(jnp.int32, (pair_w, blk), 0)
    blk_row = lax.broadcasted_iota(jnp.int32, (n_blk, blk), 0)
    tail_row = lax.broadcasted_iota(jnp.int32, (pair_w - n_blk, blk), 0)
    tail = jnp.where(tail_row == 0, NEG_INF, 0.0)
    for h in range(N_HEADS):
        bias = _select_bias(sc[h * n_blk:(h + 1) * n_blk], i, n_blk)
        bias = jnp.where(blk_row == i, 0.0, bias)
        qpair = qt[(h // 2) * pair_w:(h // 2 + 1) * pair_w, :]
        mine = (pair_row < HEAD_DIM) if h % 2 == 0 else (pair_row >= HEAD_DIM)
        qh = jnp.where(mine, qpair, jnp.zeros_like(qpair))
        qaug_scr[h] = jnp.concatenate([qh, jnp.concatenate([bias, tail], axis=0).astype(BF16)], axis=0)

    lane_blk = lax.broadcasted_iota(jnp.int32, (blk, pair_w), 1)

    def keys_aug(j, bias_row):
        start = pl.multiple_of(j * blk, blk)
        onehot = jnp.where(lane_blk == bias_row, 1.0, 0.0).astype(BF16)
        return [jnp.concatenate([kb_ref[0, pl.ds(start, blk), p * pair_w:(p + 1) * pair_w], onehot], axis=1)
                for p in range(N_HEADS // 2)]

    def scores(kj, h):
        return jnp.dot(kj[h // 2], qaug_scr[h], preferred_element_type=F32)

    ones_rows = jnp.ones((ACC_ROWS - HEAD_DIM, blk), BF16)

    def pv_and_sum(h, v_blk, p):
        vt_h = vtb_ref[0, v_blk, h * HEAD_DIM:(h + 1) * HEAD_DIM, :]
        return jnp.dot(jnp.concatenate([vt_h, ones_rows], axis=0), p.astype(BF16), preferred_element_type=F32)

    def first_block(h, st, v_blk):
        m0 = jnp.max(st, axis=0, keepdims=True)
        m_scr[h] = m0
        acc_scr[h] = pv_and_sum(h, v_blk, jnp.exp2(st - m0))

    def next_block(h, sj, v_blk):
        m_old = m_scr[h]
        m_new = jnp.maximum(m_old, jnp.max(sj, axis=0, keepdims=True))
        acc_scr[h] = jnp.exp2(m_old - m_new) * acc_scr[h] + pv_and_sum(h, v_blk, jnp.exp2(sj - m_new))
        m_scr[h] = m_new

    key_i = lax.broadcasted_iota(jnp.int32, (blk, blk), 0)
    qry_i = lax.broadcasted_iota(jnp.int32, (blk, blk), 1)
    causal = key_i <= qry_i
    ka = keys_aug(i, i)
    for h in range(N_HEADS):
        sb_scr[h] = scores(ka, h)
    k0 = keys_aug(0, 0)
    for h in range(N_HEADS):
        sa_scr[h] = scores(k0, h)
        first_block(h, jnp.where(causal, sb_scr[h], NEG_INF), i)

    def make_body(unroll):
        def body(t, base):
            for u in range(unroll):
                cur = base + t * unroll + u
                nxt = cur + 1
                nxt_keys = jnp.minimum(nxt, i - 1)
                k_aug = keys_aug(nxt_keys, jnp.where(nxt < i, nxt, n_blk))
                read, write = (sa_scr, sb_scr) if u % 2 == 0 else (sb_scr, sa_scr)
                for h in range(N_HEADS):
                    write[h] = scores(k_aug, h)
                    next_block(h, read[h], jnp.minimum(cur, i - 1))
            return base
        return body

    before_loops()
    n_wide = i // KV_UNROLL
    lax.fori_loop(0, n_wide, make_body(KV_UNROLL), 0)
    done = n_wide * KV_UNROLL
    lax.fori_loop(0, (i - done + 1) // 2, make_body(2), done)

    outs = []
    for h in range(N_HEADS):
        acc = acc_scr[h]
        outs.append(acc[0:HEAD_DIM] / acc[HEAD_DIM:HEAD_DIM + 1])
    o_ref[0] = jnp.concatenate(outs, axis=0).T.astype(BF16)


def _attn_sample_seq(q, k_new, v_new, kt, vt):
    n_new = q.shape[0]
    n_full = kt.shape[1] // MOBA_BLOCK
    tiled = jnp.concatenate([q] * N_HEADS, axis=0)
    row_head = lax.broadcasted_iota(jnp.int32, tiled.shape, 0) // n_new
    col_head = lax.broadcasted_iota(jnp.int32, tiled.shape, 1) // HEAD_DIM
    head_lanes = row_head == col_head
    q_bd = jnp.where(head_lanes, tiled, 0.0).astype(BF16)

    s_past = jnp.dot(q_bd, kt, preferred_element_type=F32)
    nt_dims = (((1,), (1,)), ((), ()))
    s_new = lax.dot_general(q_bd, k_new.astype(BF16), nt_dims, preferred_element_type=F32)
    row_step = lax.broadcasted_iota(jnp.int32, s_new.shape, 0) % n_new
    col_step = lax.broadcasted_iota(jnp.int32, s_new.shape, 1)
    s_new = jnp.where(col_step <= row_step, s_new, NEG_INF)

    blocks = [s_past[:, n * MOBA_BLOCK:(n + 1) * MOBA_BLOCK] for n in range(n_full)]
    score = [jnp.sum(sb, axis=1, keepdims=True) for sb in blocks]
    n_sel = min(MOBA_TOPK, n_full)
    lane = lax.broadcasted_iota(jnp.int32, (tiled.shape[0], LANES), 1)
    by_lane = jnp.full(lane.shape, -jnp.inf, F32)
    for n in range(n_full):
        by_lane = jnp.where(lane == n, score[n], by_lane)
    masked = []
    for n in range(n_full):
        ahead = (by_lane > score[n]) | ((lane < n) & (by_lane == score[n]))
        rank = jnp.sum(jnp.where(ahead, 1.0, 0.0), axis=1, keepdims=True)
        masked.append(blocks[n] + jnp.where(rank < n_sel, 0.0, NEG_INF))

    m = jnp.max(s_new, axis=1, keepdims=True)
    for sb in masked:
        m = jnp.maximum(m, jnp.max(sb, axis=1, keepdims=True))
    p_new = jnp.exp(s_new - m)
    l = jnp.sum(p_new, axis=1, keepdims=True)
    probs = []
    for sb in masked:
        pb = jnp.exp(sb - m)
        l = l + jnp.sum(pb, axis=1, keepdims=True)
        probs.append(pb.astype(BF16))
    p_past = jnp.concatenate(probs, axis=1)
    out = lax.dot_general(p_past, vt, nt_dims, preferred_element_type=F32)
    out = out + jnp.dot(p_new.astype(BF16), v_new.astype(BF16), preferred_element_type=F32)
    out = jnp.where(head_lanes, out / l, 0.0)
    res = out[0:n_new]
    for hh in range(1, N_HEADS):
        res = res + out[hh * n_new:(hh + 1) * n_new]
    return res


def _attn_kernel(layer, pt_ref, qt_ref, kb_ref, vtb_ref, mean_ref, qs_ref, ks_ref, vs_ref, ck_hbm, cv_hbm,
                 o_ref, os_ref, qaug_scr, sa_scr, sb_scr, m_scr, acc_scr, kbuf, vbuf, sem):
    step = pl.program_id(0) * pl.num_programs(1) + pl.program_id(1)
    n_slots = kbuf.shape[0]
    per_step = n_slots - 1
    n_seq = pt_ref.shape[0]
    n_pages = pt_ref.shape[1]
    page = ck_hbm.shape[3]
    n_new = qs_ref.shape[0] // per_step
    first = step * per_step

    def page_copies(seq):
        slot = lax.rem(seq, n_slots)
        copies = []
        for pg in range(n_pages):
            phys = pt_ref[seq, pg]
            win = pl.ds(pg * page, page)
            copies.append(pltpu.make_async_copy(ck_hbm.at[layer, phys], kbuf.at[slot, :, win], sem.at[0, slot]))
            copies.append(pltpu.make_async_copy(cv_hbm.at[layer, phys], vbuf.at[slot, :, win], sem.at[1, slot]))
        return copies

    def start_pages(seq):
        @pl.when(seq < n_seq)
        def _():
            for c in page_copies(seq):
                c.start()

    @pl.when(step == 0)
    def _():
        for j in range(per_step):
            for c in page_copies(j):
                c.start()

    start_pages(first + per_step)
    for j in range(per_step):
        for c in page_copies(first + j):
            c.wait()
    for j in range(per_step):
        rows = slice(j * n_new, (j + 1) * n_new)
        slot = lax.rem(first + j, n_slots)
        os_ref[rows, :] = _attn_sample_seq(qs_ref[rows, :], ks_ref[rows, :], vs_ref[rows, :],
                                           kbuf[slot].astype(BF16), vbuf[slot].astype(BF16))

    def start_rest_of_next_step():
        for j in range(1, per_step):
            start_pages(first + per_step + j)

    _attn_prompt_tile(pl.program_id(1), qt_ref, kb_ref, vtb_ref, mean_ref, o_ref,
                      qaug_scr, sa_scr, sb_scr, m_scr, acc_scr, start_rest_of_next_step)


def _attn(layer, page_table, qt, kb, vtb, means, q_s, k_s, v_s, ck_t, cv_t):
    bsz, n_blk, width, blk = qt.shape
    seq = kb.shape[1]
    assert n_blk < 2 * HEAD_DIM
    n_seq, n_pages = page_table.shape
    n_tok = q_s.shape[0]
    n_new = n_tok // n_seq
    page = ck_t.shape[3]
    past = n_pages * page
    assert past % MOBA_BLOCK == 0 and past >= MOBA_BLOCK and ck_t.shape[2] == width
    n_steps = bsz * n_blk
    assert n_seq % n_steps == 0
    per_step = n_seq // n_steps
    sample_tile = pl.BlockSpec((per_step * n_new, width), lambda b, i, pt: (b * n_blk + i, 0))
    grid_spec = pltpu.PrefetchScalarGridSpec(
        num_scalar_prefetch=1,
        grid=(bsz, n_blk),
        in_specs=[
            pl.BlockSpec((1, 1, width, blk), lambda b, i, pt: (b, i, 0, 0)),
            pl.BlockSpec((1, seq, width), lambda b, i, pt: (b, 0, 0), pipeline_mode=pl.Buffered(1)),
            pl.BlockSpec((1, n_blk, width, blk), lambda b, i, pt: (b, 0, 0, 0), pipeline_mode=pl.Buffered(1)),
            pl.BlockSpec((1, n_blk, width), lambda b, i, pt: (b, 0, 0)),
            sample_tile, sample_tile, sample_tile,
            pl.BlockSpec(memory_space=pl.ANY), pl.BlockSpec(memory_space=pl.ANY),
        ],
        out_specs=(pl.BlockSpec((1, blk, width), lambda b, i, pt: (b, i, 0)), sample_tile),
        scratch_shapes=[pltpu.VMEM((N_HEADS, 4 * HEAD_DIM, blk), BF16),
                        pltpu.VMEM((N_HEADS, blk, blk), F32),
                        pltpu.VMEM((N_HEADS, blk, blk), F32),
                        pltpu.VMEM((N_HEADS, 1, blk), F32),
                        pltpu.VMEM((N_HEADS, ACC_ROWS, blk), F32),
                        pltpu.VMEM((per_step + 1, width, past), F32),
                        pltpu.VMEM((per_step + 1, width, past), F32),
                        pltpu.SemaphoreType.DMA((2, per_step + 1))],
    )
    return pl.pallas_call(
        functools.partial(_attn_kernel, layer),
        grid_spec=grid_spec,
        out_shape=(jax.ShapeDtypeStruct((bsz, seq, width), BF16), jax.ShapeDtypeStruct((n_tok, width), F32)),
        compiler_params=_params(2),
        name="attn",
    )(page_table, qt, kb, vtb, means, q_s, k_s, v_s, ck_t, cv_t)


def _merge_kernel(x_ref, attn_ref, sga_ref, sig_ref, yb_ref, wpa_ref, wo_ref, o_ref):
    gated = attn_ref[...].astype(BF16) * sga_ref[...]
    a = jnp.dot(gated, wpa_ref[...], preferred_element_type=F32)
    y = sig_ref[...].astype(F32) * a + yb_ref[...].astype(F32)
    o_ref[...] = x_ref[...] + jnp.dot(y.astype(BF16), wo_ref[...], preferred_element_type=F32)


def _merge(layer, x, attn, sga, sig, yb, wpa, wo, tm):
    n, d_model = x.shape
    assert n % tm == 0
    tile = lambda width: pl.BlockSpec((tm, width), lambda t: (t, 0))
    return pl.pallas_call(
        _merge_kernel,
        grid=(n // tm,),
        in_specs=[tile(d_model), tile(ATTN_WIDTH), tile(ATTN_WIDTH), tile(d_model), tile(d_model),
                  _layer_slice(wpa, layer), _layer_slice(wo, layer)],
        out_specs=tile(d_model),
        out_shape=jax.ShapeDtypeStruct((n, d_model), F32),
        compiler_params=_params(1),
        name="merge",
    )(x, attn, sga, sig, yb, wpa, wo)


def _proj_sample_kernel(past_len, x_ref, nw_ref, win_ref, ones_ref, cos_ref, sin_ref, qw_ref, kw_ref,
                        wgrp_ref, pscale_ref, wpb_ref, state_ref,
                        q_out, k_out, v_out, kst_out, vst_out, sga_out, sig_out, yb_out, pool_out,
                        slab_scr):
    n_tok, d_model = x_ref.shape
    n_seq = state_ref.shape[1]
    n_new = n_tok // n_seq
    n_slab = slab_scr.shape[0]
    offs = _col_ranges()
    h = _rms_norm_rows(x_ref[...], nw_ref[...]).astype(BF16)

    def proj(lo, hi):
        return jnp.dot(h, win_ref[:, lo:hi], preferred_element_type=F32)

    def to_slabs(val):
        for c in range(n_slab):
            slab_scr[c] = val[:, c * LANES:(c + 1) * LANES]

    def step_rows(s):
        return jnp.concatenate(
            [slab_scr[c, pl.ds(s, n_seq, stride=n_new), :] for c in range(n_slab)], axis=-1)

    ones_bd = ones_ref[...]
    cos_t = cos_ref[...]
    sin_t = sin_ref[...]
    q_out[...] = _head_norm_rope(proj(offs[0], offs[1]), ones_bd, cos_t, sin_t, qw_ref[...]) * ATTN_SCALE

    k = _head_norm_rope(proj(offs[1], offs[2]), ones_bd, cos_t, sin_t, kw_ref[...])
    k_out[...] = k
    to_slabs(k)
    for s in range(n_new):
        kst_out[s] = step_rows(s).T

    v = proj(offs[2], offs[3])
    v_out[...] = v
    to_slabs(v)
    for s in range(n_new):
        vst_out[s] = step_rows(s).T

    sga_out[...] = _silu(proj(offs[3], offs[4])).astype(BF16)

    zu = proj(offs[4], offs[5])
    to_slabs(zu)
    hist = [state_ref[j] for j in range(POOL_BUF)] + [step_rows(s) for s in range(n_new)]
    for j in range(POOL_BUF):
        pool_out[j] = hist[len(hist) - POOL_BUF + j]
    for s in range(n_new):
        cur = POOL_BUF + s
        parts = []
        for g, w in enumerate(POOL_WINDOWS):
            cols = slice(g * POOL_GROUP_WIDTH, (g + 1) * POOL_GROUP_WIDTH)
            acc = hist[cur][:, cols]
            for back in range(1, w):
                acc = acc + hist[cur - back][:, cols]
            cnt = float(min(w, past_len + s + 1))
            parts.append(acc / cnt - hist[cur][:, cols])
        ds = jnp.concatenate(parts, axis=-1)
        for c in range(n_slab):
            slab_scr[c, pl.ds(s, n_seq, stride=n_new), :] = ds[:, c * LANES:(c + 1) * LANES]
    d = jnp.concatenate([slab_scr[c] for c in range(n_slab)], axis=-1)

    zgp = proj(offs[5], offs[6])
    zgb = proj(offs[6] + d_model, offs[6] + 2 * d_model)
    yb_out[...] = _pool_project(d, zgp, zgb, wgrp_ref, pscale_ref, wpb_ref).astype(BF16)
    sig_out[...] = jax.nn.sigmoid(proj(offs[6], offs[6] + d_model)).astype(BF16)


def _proj_sample(layer, x, nw, win, ones_bd, cos_t, sin_t, qw, kw, wgrp, pscale, wpb, state_t, past_len):
    n_tok, d_model = x.shape
    _, n_hist, n_seq, pool_w = state_t.shape
    assert n_hist == POOL_BUF and pool_w == POOL_WIDTH and n_tok % n_seq == 0
    n_new = n_tok // n_seq
    assert ATTN_WIDTH == POOL_WIDTH
    n_slab = POOL_WIDTH // LANES
    full = lambda shape: pl.BlockSpec(shape, lambda t: (0,) * len(shape))
    out_shape = (
        jax.ShapeDtypeStruct((n_tok, ATTN_WIDTH), F32),
        jax.ShapeDtypeStruct((n_tok, ATTN_WIDTH), F32),
        jax.ShapeDtypeStruct((n_tok, ATTN_WIDTH), F32),
        jax.ShapeDtypeStruct((n_new, ATTN_WIDTH, n_seq), F32),
        jax.ShapeDtypeStruct((n_new, ATTN_WIDTH, n_seq), F32),
        jax.ShapeDtypeStruct((n_tok, ATTN_WIDTH), BF16),
        jax.ShapeDtypeStruct((n_tok, d_model), BF16),
        jax.ShapeDtypeStruct((n_tok, d_model), BF16),
        jax.ShapeDtypeStruct((POOL_BUF, n_seq, POOL_WIDTH), F32),
    )
    args = (x, nw, win, ones_bd, cos_t, sin_t, qw, kw, wgrp, pscale, wpb, state_t)
    return pl.pallas_call(
        functools.partial(_proj_sample_kernel, past_len),
        grid=(1,),
        in_specs=[_layer_slice(a, layer) if any(a is w for w in (win, wgrp, wpb, state_t)) else full(a.shape)
                  for a in args],
        out_specs=tuple(full(o.shape) for o in out_shape),
        out_shape=out_shape,
        scratch_shapes=[pltpu.VMEM((n_slab, n_tok, LANES), F32)],
        compiler_params=_params(1),
        name="proj_sample",
    )(*args)


def _rope_tables(pos):
    half = HEAD_DIM // 2
    inv_freq = jnp.exp(-math.log(ROPE_THETA) * jnp.arange(half, dtype=F32) / half)
    ang = pos.astype(F32)[:, None] * inv_freq[None, :]
    cos, sin = jnp.cos(ang), jnp.sin(ang)
    reps = LANES // HEAD_DIM
    return (jnp.tile(jnp.concatenate([cos, cos], axis=-1), (1, reps)),
            jnp.tile(jnp.concatenate([-sin, sin], axis=-1), (1, reps)))


def _norm_rows(w):
    half = HEAD_DIM // 2
    reps = LANES // HEAD_DIM
    swapped = jnp.concatenate([w[half:], w[:half]])
    return jnp.stack([jnp.tile(w, reps), jnp.tile(swapped, reps)]).astype(F32)


def kernel(x_prompt, x_sample, cache_k, cache_v, state_pool, page_table, norm_w, w_in, q_norm_w,
           k_norm_w, w_pool_grp, pool_scale, w_proj_attn, w_proj_pool, w_out):
    bp, sp, d_model = x_prompt.shape
    bs, ss, _ = x_sample.shape
    depth, n_phys, page, n_heads, head_dim = cache_k.shape
    assert (n_heads, head_dim) == (N_HEADS, HEAD_DIM) and page == LANES
    n_pages = page_table.shape[1]
    past_len = n_pages * page

    cos_p, sin_p = _rope_tables(jnp.arange(sp, dtype=jnp.int32))
    cos_s, sin_s = _rope_tables(past_len + jnp.arange(ss, dtype=jnp.int32))
    cos_s, sin_s = jnp.tile(cos_s, (bs, 1)), jnp.tile(sin_s, (bs, 1))
    ones_bd = jnp.kron(jnp.eye(N_HEADS, dtype=F32), jnp.ones((HEAD_DIM, HEAD_DIM), F32)).astype(BF16)

    ck_t = cache_k.transpose(0, 1, 3, 4, 2).reshape(depth, n_phys, ATTN_WIDTH, page)
    cv_t = cache_v.transpose(0, 1, 3, 4, 2).reshape(depth, n_phys, ATTN_WIDTH, page)
    state_t = state_pool.transpose(0, 2, 1, 3)

    w_in_b = w_in.astype(BF16)
    w_grp_b = w_pool_grp.astype(BF16)
    w_pa_b = w_proj_attn.astype(BF16)
    w_pb_b = w_proj_pool.astype(BF16)
    w_o_b = w_out.astype(BF16)

    xp = x_prompt
    xs = x_sample.reshape(bs * ss, d_model)
    kv_pages = []
    pp_l, ks_l, vs_l, ps_l = [], [], [], []
    for l in range(depth):
        nw = norm_w[l][None, :]
        qw, kw = _norm_rows(q_norm_w[l]), _norm_rows(k_norm_w[l])
        pscale = pool_scale[l][None, :]

        qt, kb, kt, means, vtf, vtb, sga, sig, yb, plast = _proj_prompt(
            l, kv_pages if l == depth - 1 else [], xp, nw, w_in_b, ones_bd, cos_p, sin_p, qw, kw,
            w_grp_b, pscale, w_pb_b)
        kv_pages += [kt, vtf]
        pp_l.append(plast)
        q_s, k_s, v_s, kst, vst, sga_s, sig_s, yb_s, pool_s = _proj_sample(
            l, xs, nw, w_in_b, ones_bd, cos_s, sin_s, qw, kw, w_grp_b, pscale, w_pb_b, state_t, past_len)

        attn, attn_s = _attn(l, page_table, qt, kb, vtb, means.reshape(bp, -1, ATTN_WIDTH),
                             q_s, k_s, v_s, ck_t, cv_t)

        flat = lambda a: a.reshape(bp * sp, a.shape[-1])
        xp = _merge(l, flat(xp), flat(attn), flat(sga), flat(sig), flat(yb), w_pa_b, w_o_b,
                    MERGE_TILE).reshape(bp, sp, d_model)
        xs = _merge(l, xs, attn_s, sga_s, sig_s, yb_s, w_pa_b, w_o_b, bs * ss)
        to_steps = lambda a: a.reshape(ss, N_HEADS, HEAD_DIM, bs).transpose(3, 0, 1, 2)
        ks_l.append(to_steps(kst))
        vs_l.append(to_steps(vst))
        ps_l.append(pool_s.transpose(1, 0, 2))

    to_pages = lambda a: a.reshape(depth, bp, sp // page, N_HEADS, HEAD_DIM, page).transpose(0, 1, 2, 5, 3, 4)
    return (xp, xs.reshape(bs, ss, d_model), to_pages(kv_pages[-2]), to_pages(kv_pages[-1]), jnp.stack(pp_l),
            jnp.stack(ks_l), jnp.stack(vs_l), jnp.stack(ps_l))
```

```python
import functools
import math

import jax
import jax.numpy as jnp
from jax import lax
from jax.experimental import pallas as pl
from jax.experimental.pallas import tpu as pltpu

F32 = jnp.float32
BF16 = jnp.bfloat16

N_HEADS = 8
HEAD_DIM = 64
ATTN_WIDTH = N_HEADS * HEAD_DIM
MOBA_BLOCK = 256
MOBA_TOPK = 3
POOL_WINDOWS = (2, 4, 8, 16)
POOL_GROUP_WIDTH = 128
POOL_WIDTH = len(POOL_WINDOWS) * POOL_GROUP_WIDTH
POOL_BUF = max(POOL_WINDOWS) - 1
POOL_HALO = 16
ROPE_THETA = 10000.0
RMS_EPS = 1e-6
NEG_INF = -1e30
ATTN_SCALE = HEAD_DIM ** -0.5
LOG2_E = math.log2(math.e)
BF16_SUBLANES = 16
ACC_ROWS = HEAD_DIM + BF16_SUBLANES

LANES = 128
VMEM_LIMIT_BYTES = 56 * 1024 * 1024

PROJ_TILE = 2 * MOBA_BLOCK
MERGE_TILE = 4 * MOBA_BLOCK
KV_UNROLL = 4


def _resident(shape):
    return pl.BlockSpec(shape, lambda *_: (0,) * len(shape), pipeline_mode=pl.Buffered(1))


def _layer_slice(stacked, layer):
    rest = stacked.shape[1:]
    return pl.BlockSpec((None,) + rest, lambda *_: (layer,) + (0,) * len(rest), pipeline_mode=pl.Buffered(1))


def _params(n_axes):
    return pltpu.CompilerParams(dimension_semantics=("arbitrary",) * n_axes,
                                vmem_limit_bytes=VMEM_LIMIT_BYTES)


def _silu(z):
    return z * jax.nn.sigmoid(z)


def _rms_norm_rows(x, w_row):
    ms = jnp.mean(x * x, axis=-1, keepdims=True)
    return x * lax.rsqrt(ms + RMS_EPS) * w_row


def _head_norm_rope(z, ones_bd, cos_t, sin_t, w_rows):
    m = z.shape[0]
    ssq = jnp.dot((z * z).astype(BF16), ones_bd, preferred_element_type=F32)
    r = lax.rsqrt(ssq * (1.0 / HEAD_DIM) + RMS_EPS)
    cw = cos_t * w_rows[0:1, :]
    sw = sin_t * w_rows[1:2, :]
    lane = lax.broadcasted_iota(jnp.int32, (m, LANES), 1)
    first_half = (lane % HEAD_DIM) < (HEAD_DIM // 2)
    outs = []
    for c in range(ATTN_WIDTH // LANES):
        zc = z[:, c * LANES:(c + 1) * LANES]
        partner = jnp.where(first_half,
                            pltpu.roll(zc, LANES - HEAD_DIM // 2, 1),
                            pltpu.roll(zc, HEAD_DIM // 2, 1))
        outs.append((zc * cw + partner * sw) * r[:, c * LANES:(c + 1) * LANES])
    return jnp.concatenate(outs, axis=-1)


def _col_ranges():
    sizes = (ATTN_WIDTH,) * 4 + (POOL_WIDTH,) * 2
    offs = [0]
    for s in sizes:
        offs.append(offs[-1] + s)
    return offs


def _pool_project(d, zgp, zgb, wgrp_ref, pscale_ref, wpb_ref):
    parts = []
    for g in range(len(POOL_WINDOWS)):
        dg = d[:, g * POOL_GROUP_WIDTH:(g + 1) * POOL_GROUP_WIDTH].astype(BF16)
        parts.append(jnp.dot(dg, wgrp_ref[g], preferred_element_type=F32))
    pool = jnp.concatenate(parts, axis=-1) * pscale_ref[...]
    pg = (pool * _silu(zgp)).astype(BF16)
    b = jnp.dot(pg, wpb_ref[...], preferred_element_type=F32)
    return jax.nn.sigmoid(zgb) * b


N_PROJ_PROMPT_OUT = 10


def _proj_prompt_kernel(x_ref, nw_ref, win_ref, ones_ref, cos_ref, sin_ref, qw_ref, kw_ref,
                        wgrp_ref, pscale_ref, wpb_ref, *rest):
    (qt_out, kb_out, kt_out, mean_out, vtf_out, vtb_out,
     sga_out, sig_out, yb_out, plast_out, ubuf) = rest[-(N_PROJ_PROMPT_OUT + 1):]
    earlier = rest[:-(N_PROJ_PROMPT_OUT + 1)]
    own = len(earlier) // 2
    for slot in range(own):
        kt_out[slot] = earlier[2 * slot][0]
        vtf_out[slot] = earlier[2 * slot + 1][0]
    t = pl.program_id(1)
    tm = x_ref.shape[1]
    d_model = x_ref.shape[2]
    offs = _col_ranges()

    @pl.when((pl.program_id(0) == 0) & (t == 0))
    def _():
        ubuf[0:POOL_HALO, :] = jnp.zeros((POOL_HALO, POOL_WIDTH), F32)

    h = _rms_norm_rows(x_ref[0], nw_ref[...]).astype(BF16)

    def proj(lo, hi):
        return jnp.dot(h, win_ref[:, lo:hi], preferred_element_type=F32)

    ones_bd = ones_ref[...]
    cos_t = cos_ref[...]
    sin_t = sin_ref[...]
    n_blk = tm // MOBA_BLOCK
    n_page = tm // LANES

    q = _head_norm_rope(proj(offs[0], offs[1]), ones_bd, cos_t, sin_t, qw_ref[...]) * (ATTN_SCALE * LOG2_E)
    qt = q.T.astype(BF16)
    for i in range(n_blk):
        qt_out[0, i] = qt[:, i * MOBA_BLOCK:(i + 1) * MOBA_BLOCK]

    k = _head_norm_rope(proj(offs[1], offs[2]), ones_bd, cos_t, sin_t, kw_ref[...])
    kb_out[0] = k.astype(BF16)
    kt = k.T
    for i in range(n_page):
        kt_out[own, 0, i] = kt[:, i * LANES:(i + 1) * LANES]
    for i in range(n_blk):
        mean_out[0, i] = jnp.mean(k[i * MOBA_BLOCK:(i + 1) * MOBA_BLOCK], axis=0, keepdims=True)

    vt = proj(offs[2], offs[3]).T
    for i in range(n_page):
        vtf_out[own, 0, i] = vt[:, i * LANES:(i + 1) * LANES]
    vtb = vt.astype(BF16)
    for i in range(n_blk):
        vtb_out[0, i] = vtb[:, i * MOBA_BLOCK:(i + 1) * MOBA_BLOCK]

    sga_out[0] = _silu(proj(offs[3], offs[4])).astype(BF16)

    zu = proj(offs[4], offs[5])
    ubuf[POOL_HALO:POOL_HALO + tm, :] = zu
    pos = t * tm + lax.broadcasted_iota(jnp.int32, (tm, 1), 0)
    parts = []
    for g, w in enumerate(POOL_WINDOWS):
        cols = slice(g * POOL_GROUP_WIDTH, (g + 1) * POOL_GROUP_WIDTH)
        zug = zu[:, cols]
        acc = zug
        for back in range(1, w):
            acc = acc + ubuf[pl.ds(POOL_HALO - back, tm), cols]
        cnt = jnp.minimum(w, pos + 1).astype(F32)
        parts.append(acc / cnt - zug)
    d = jnp.concatenate(parts, axis=-1)
    last = t == pl.num_programs(1) - 1
    ubuf[0:POOL_HALO, :] = jnp.where(last, 0.0, zu[tm - POOL_HALO:tm, :])
    plast_out[0] = zu[tm - POOL_BUF:tm, :]

    zgp = proj(offs[5], offs[6])
    zgb = proj(offs[6] + d_model, offs[6] + 2 * d_model)
    yb_out[0] = _pool_project(d, zgp, zgb, wgrp_ref, pscale_ref, wpb_ref).astype(BF16)
    sig_out[0] = jax.nn.sigmoid(proj(offs[6], offs[6] + d_model)).astype(BF16)


def _proj_prompt(layer, earlier_pages, x, nw, win, ones_bd, cos_t, sin_t, qw, kw, wgrp, pscale, wpb):
    bsz, seq, d_model = x.shape
    tm = PROJ_TILE
    assert seq % tm == 0 and tm % MOBA_BLOCK == 0
    n_t = seq // tm
    n_blk, n_page = seq // MOBA_BLOCK, seq // LANES
    n_slots = len(earlier_pages) // 2 + 1
    tile = lambda width: pl.BlockSpec((1, tm, width), lambda b, t: (b, t, 0))
    paged = lambda per, minor: pl.BlockSpec((1, per, ATTN_WIDTH, minor), lambda b, t: (b, t, 0, 0))
    pages = lambda slots: pl.BlockSpec((slots, 1, tm // LANES, ATTN_WIDTH, LANES), lambda b, t: (0, b, t, 0, 0))
    out_shape = (
        jax.ShapeDtypeStruct((bsz, n_blk, ATTN_WIDTH, MOBA_BLOCK), BF16),
        jax.ShapeDtypeStruct((bsz, seq, ATTN_WIDTH), BF16),
        jax.ShapeDtypeStruct((n_slots, bsz, n_page, ATTN_WIDTH, LANES), F32),
        jax.ShapeDtypeStruct((bsz, n_blk, 1, ATTN_WIDTH), F32),
        jax.ShapeDtypeStruct((n_slots, bsz, n_page, ATTN_WIDTH, LANES), F32),
        jax.ShapeDtypeStruct((bsz, n_blk, ATTN_WIDTH, MOBA_BLOCK), BF16),
        jax.ShapeDtypeStruct((bsz, seq, ATTN_WIDTH), BF16),
        jax.ShapeDtypeStruct((bsz, seq, d_model), BF16),
        jax.ShapeDtypeStruct((bsz, seq, d_model), BF16),
        jax.ShapeDtypeStruct((bsz, POOL_BUF, POOL_WIDTH), F32),
    )
    out_specs = (
        paged(tm // MOBA_BLOCK, MOBA_BLOCK),
        tile(ATTN_WIDTH),
        pages(n_slots),
        pl.BlockSpec((1, tm // MOBA_BLOCK, 1, ATTN_WIDTH), lambda b, t: (b, t, 0, 0)),
        pages(n_slots),
        paged(tm // MOBA_BLOCK, MOBA_BLOCK),
        tile(ATTN_WIDTH),
        tile(d_model),
        tile(d_model),
        pl.BlockSpec((1, POOL_BUF, POOL_WIDTH), lambda b, t: (b, 0, 0)),
    )
    in_specs = [
        tile(d_model),
        _resident(nw.shape), _layer_slice(win, layer), _resident(ones_bd.shape),
        pl.BlockSpec((tm, LANES), lambda b, t: (t, 0)),
        pl.BlockSpec((tm, LANES), lambda b, t: (t, 0)),
        _resident(qw.shape), _resident(kw.shape),
        _layer_slice(wgrp, layer), _resident(pscale.shape), _layer_slice(wpb, layer),
    ]
    in_specs += [pages(1)] * len(earlier_pages)
    args = [x, nw, win, ones_bd, cos_t, sin_t, qw, kw, wgrp, pscale, wpb, *earlier_pages]
    return pl.pallas_call(
        _proj_prompt_kernel,
        grid=(bsz, n_t),
        in_specs=in_specs,
        out_specs=out_specs,
        out_shape=out_shape,
        scratch_shapes=[pltpu.VMEM((POOL_HALO + tm, POOL_WIDTH), F32)],
        compiler_params=_params(2),
        name="proj_prompt",
    )(*args)


def _select_bias(sc, n_valid, n_blk):
    jrow = lax.broadcasted_iota(jnp.int32, sc.shape, 0)
    jrow_f = jrow.astype(F32)
    valid = jrow < n_valid
    s = jnp.where(valid, sc, -jnp.inf)
    for _ in range(MOBA_TOPK):
        top = jnp.max(s, axis=0, keepdims=True)
        first = jnp.min(jnp.where(s == top, jrow_f, float(n_blk)), axis=0, keepdims=True)
        s = jnp.where(jrow_f == first, -jnp.inf, s)
    return jnp.where(valid & (s == -jnp.inf), 0.0, NEG_INF)


def _attn_prompt_tile(i, qt_ref, kb_ref, vtb_ref, mean_ref, o_ref, qaug_scr, sa_scr, sb_scr, m_scr, acc_scr,
                      before_loops):
    n_blk = mean_ref.shape[1]
    blk = MOBA_BLOCK
    pair_w = 2 * HEAD_DIM
    qt = qt_ref[0, 0]

    means = mean_ref[0].astype(BF16)
    tiled = jnp.concatenate([means] * N_HEADS, axis=0)
    row_head = lax.broadcasted_iota(jnp.int32, tiled.shape, 0) // n_blk
    col_head = lax.broadcasted_iota(jnp.int32, tiled.shape, 1) // HEAD_DIM
    means_bd = jnp.where(row_head == col_head, tiled, jnp.zeros_like(tiled))
    sc = jnp.dot(means_bd, qt, preferred_element_type=F32)

    pair_row = lax.broadcasted_iota(jnp.int32, (pair_w, blk), 0)
    blk_row = lax.broadcasted_iota(jnp.int32, (n_blk, blk), 0)
    tail_row = lax.broadcasted_iota(jnp.int32, (pair_w - n_blk, blk), 0)
    tail = jnp.where(tail_row == 0, NEG_INF, 0.0)
    for h in range(N_HEADS):
        bias = _select_bias(sc[h * n_blk:(h + 1) * n_blk], i, n_blk)
        bias = jnp.where(blk_row == i, 0.0, bias)
        qpair = qt[(h // 2) * pair_w:(h // 2 + 1) * pair_w, :]
        mine = (pair_row < HEAD_DIM) if h % 2 == 0 else (pair_row >= HEAD_DIM)
        qh = jnp.where(mine, qpair, jnp.zeros_like(qpair))
        qaug_scr[h] = jnp.concatenate([qh, jnp.concatenate([bias, tail], axis=0).astype(BF16)], axis=0)

    lane_blk = lax.broadcasted_iota(jnp.int32, (blk, pair_w), 1)

    def keys_aug(j, bias_row):
        start = pl.multiple_of(j * blk, blk)
        onehot = jnp.where(lane_blk == bias_row, 1.0, 0.0).astype(BF16)
        return [jnp.concatenate([kb_ref[0, pl.ds(start, blk), p * pair_w:(p + 1) * pair_w], onehot], axis=1)
                for p in range(N_HEADS // 2)]

    def scores(kj, h):
        return jnp.dot(kj[h // 2], qaug_scr[h], preferred_element_type=F32)

    ones_rows = jnp.ones((ACC_ROWS - HEAD_DIM, blk), BF16)

    def pv_and_sum(h, v_blk, p):
        vt_h = vtb_ref[0, v_blk, h * HEAD_DIM:(h + 1) * HEAD_DIM, :]
        return jnp.dot(jnp.concatenate([vt_h, ones_rows], axis=0), p.astype(BF16), preferred_element_type=F32)

    def first_block(h, st, v_blk):
        m0 = jnp.max(st, axis=0, keepdims=True)
        m_scr[h] = m0
        acc_scr[h] = pv_and_sum(h, v_blk, jnp.exp2(st - m0))

    def next_block(h, sj, v_blk):
        m_old = m_scr[h]
        m_new = jnp.maximum(m_old, jnp.max(sj, axis=0, keepdims=True))
        acc_scr[h] = jnp.exp2(m_old - m_new) * acc_scr[h] + pv_and_sum(h, v_blk, jnp.exp2(sj - m_new))
        m_scr[h] = m_new

    key_i = lax.broadcasted_iota(jnp.int32, (blk, blk), 0)
    qry_i = lax.broadcasted_iota(jnp.int32, (blk, blk), 1)
    causal = key_i <= qry_i
    ka = keys_aug(i, i)
    for h in range(N_HEADS):
        sb_scr[h] = scores(ka, h)
    k0 = keys_aug(0, 0)
    for h in range(N_HEADS):
        sa_scr[h] = scores(k0, h)
        first_block(h, jnp.where(causal, sb_scr[h], NEG_INF), i)

    def make_body(unroll):
        def body(t, base):
            for u in range(unroll):
                cur = base + t * unroll + u
                nxt = cur + 1
                nxt_keys = jnp.minimum(nxt, i - 1)
                k_aug = keys_aug(nxt_keys, jnp.where(nxt < i, nxt, n_blk))
                read, write = (sa_scr, sb_scr) if u % 2 == 0 else (sb_scr, sa_scr)
                for h in range(N_HEADS):
                    write[h] = scores(k_aug, h)
                    next_block(h, read[h], jnp.minimum(cur, i - 1))
            return base
        return body

    before_loops()
    n_wide = i // KV_UNROLL
    lax.fori_loop(0, n_wide, make_body(KV_UNROLL), 0)
    done = n_wide * KV_UNROLL
    lax.fori_loop(0, (i - done + 1) // 2, make_body(2), done)

    outs = []
    for h in range(N_HEADS):
        acc = acc_scr[h]
        outs.append(acc[0:HEAD_DIM] / acc[HEAD_DIM:HEAD_DIM + 1])
    o_ref[0] = jnp.concatenate(outs, axis=0).T.astype(BF16)


def _attn_sample_seq(q, k_new, v_new, kt, vt):
    n_new = q.shape[0]
    n_full = kt.shape[1] // MOBA_BLOCK
    tiled = jnp.concatenate([q] * N_HEADS, axis=0)
    row_head = lax.broadcasted_iota(jnp.int32, tiled.shape, 0) // n_new
    col_head = lax.broadcasted_iota(jnp.int32, tiled.shape, 1) // HEAD_DIM
    head_lanes = row_head == col_head
    q_bd = jnp.where(head_lanes, tiled, 0.0).astype(BF16)

    s_past = jnp.dot(q_bd, kt, preferred_element_type=F32)
    nt_dims = (((1,), (1,)), ((), ()))
    s_new = lax.dot_general(q_bd, k_new.astype(BF16), nt_dims, preferred_element_type=F32)
    row_step = lax.broadcasted_iota(jnp.int32, s_new.shape, 0) % n_new
    col_step = lax.broadcasted_iota(jnp.int32, s_new.shape, 1)
    s_new = jnp.where(col_step <= row_step, s_new, NEG_INF)

    blocks = [s_past[:, n * MOBA_BLOCK:(n + 1) * MOBA_BLOCK] for n in range(n_full)]
    score = [jnp.sum(sb, axis=1, keepdims=True) for sb in blocks]
    n_sel = min(MOBA_TOPK, n_full)
    lane = lax.broadcasted_iota(jnp.int32, (tiled.shape[0], LANES), 1)
    by_lane = jnp.full(lane.shape, -jnp.inf, F32)
    for n in range(n_full):
        by_lane = jnp.where(lane == n, score[n], by_lane)
    masked = []
    for n in range(n_full):
        ahead = (by_lane > score[n]) | ((lane < n) & (by_lane == score[n]))
        rank = jnp.sum(jnp.where(ahead, 1.0, 0.0), axis=1, keepdims=True)
        masked.append(blocks[n] + jnp.where(rank < n_sel, 0.0, NEG_INF))

    m = jnp.max(s_new, axis=1, keepdims=True)
    for sb in masked:
        m = jnp.maximum(m, jnp.max(sb, axis=1, keepdims=True))
    p_new = jnp.exp(s_new - m)
    l = jnp.sum(p_new, axis=1, keepdims=True)
    probs = []
    for sb in masked:
        pb = jnp.exp(sb - m)
        l = l + jnp.sum(pb, axis=1, keepdims=True)
        probs.append(pb.astype(BF16))
    p_past = jnp.concatenate(probs, axis=1)
    out = lax.dot_general(p_past, vt, nt_dims, preferred_element_type=F32)
    out = out + jnp.dot(p_new.astype(BF16), v_new.astype(BF16), preferred_element_type=F32)
    out = jnp.where(head_lanes, out / l, 0.0)
    res = out[0:n_new]
    for hh in range(1, N_HEADS):
        res = res + out[hh * n_new:(hh + 1) * n_new]
    return res


def _attn_kernel(layer, pt_ref, qt_ref, kb_ref, vtb_ref, mean_ref, qs_ref, ks_ref, vs_ref, ck_hbm, cv_hbm,
                 o_ref, os_ref, qaug_scr, sa_scr, sb_scr, m_scr, acc_scr, kbuf, vbuf, sem):
    step = pl.program_id(0) * pl.num_programs(1) + pl.program_id(1)
    n_slots = kbuf.shape[0]
    per_step = n_slots - 1
    n_seq = pt_ref.shape[0]
    n_pages = pt_ref.shape[1]
    page = ck_hbm.shape[3]
    n_new = qs_ref.shape[0] // per_step
    first = step * per_step

    def page_copies(seq):
        slot = lax.rem(seq, n_slots)
        copies = []
        for pg in range(n_pages):
            phys = pt_ref[seq, pg]
            win = pl.ds(pg * page, page)
            copies.append(pltpu.make_async_copy(ck_hbm.at[layer, phys], kbuf.at[slot, :, win], sem.at[0, slot]))
            copies.append(pltpu.make_async_copy(cv_hbm.at[layer, phys], vbuf.at[slot, :, win], sem.at[1, slot]))
        return copies

    def start_pages(seq):
        @pl.when(seq < n_seq)
        def _():
            for c in page_copies(seq):
                c.start()

    @pl.when(step == 0)
    def _():
        for j in range(per_step):
            for c in page_copies(j):
                c.start()

    start_pages(first + per_step)
    for j in range(per_step):
        for c in page_copies(first + j):
            c.wait()
    for j in range(per_step):
        rows = slice(j * n_new, (j + 1) * n_new)
        slot = lax.rem(first + j, n_slots)
        os_ref[rows, :] = _attn_sample_seq(qs_ref[rows, :], ks_ref[rows, :], vs_ref[rows, :],
                                           kbuf[slot].astype(BF16), vbuf[slot].astype(BF16))

    def start_rest_of_next_step():
        for j in range(1, per_step):
            start_pages(first + per_step + j)

    _attn_prompt_tile(pl.program_id(1), qt_ref, kb_ref, vtb_ref, mean_ref, o_ref,
                      qaug_scr, sa_scr, sb_scr, m_scr, acc_scr, start_rest_of_next_step)


def _attn(layer, page_table, qt, kb, vtb, means, q_s, k_s, v_s, ck_t, cv_t):
    bsz, n_blk, width, blk = qt.shape
    seq = kb.shape[1]
    assert n_blk < 2 * HEAD_DIM
    n_seq, n_pages = page_table.shape
    n_tok = q_s.shape[0]
    n_new = n_tok // n_seq
    page = ck_t.shape[3]
    past = n_pages * page
    assert past % MOBA_BLOCK == 0 and past >= MOBA_BLOCK and ck_t.shape[2] == width
    n_steps = bsz * n_blk
    assert n_seq % n_steps == 0
    per_step = n_seq // n_steps
    sample_tile = pl.BlockSpec((per_step * n_new, width), lambda b, i, pt: (b * n_blk + i, 0))
    grid_spec = pltpu.PrefetchScalarGridSpec(
        num_scalar_prefetch=1,
        grid=(bsz, n_blk),
        in_specs=[
            pl.BlockSpec((1, 1, width, blk), lambda b, i, pt: (b, i, 0, 0)),
            pl.BlockSpec((1, seq, width), lambda b, i, pt: (b, 0, 0), pipeline_mode=pl.Buffered(1)),
            pl.BlockSpec((1, n_blk, width, blk), lambda b, i, pt: (b, 0, 0, 0), pipeline_mode=pl.Buffered(1)),
            pl.BlockSpec((1, n_blk, width), lambda b, i, pt: (b, 0, 0)),
            sample_tile, sample_tile, sample_tile,
            pl.BlockSpec(memory_space=pl.ANY), pl.BlockSpec(memory_space=pl.ANY),
        ],
        out_specs=(pl.BlockSpec((1, blk, width), lambda b, i, pt: (b, i, 0)), sample_tile),
        scratch_shapes=[pltpu.VMEM((N_HEADS, 4 * HEAD_DIM, blk), BF16),
                        pltpu.VMEM((N_HEADS, blk, blk), F32),
                        pltpu.VMEM((N_HEADS, blk, blk), F32),
                        pltpu.VMEM((N_HEADS, 1, blk), F32),
                        pltpu.VMEM((N_HEADS, ACC_ROWS, blk), F32),
                        pltpu.VMEM((per_step + 1, width, past), F32),
                        pltpu.VMEM((per_step + 1, width, past), F32),
                        pltpu.SemaphoreType.DMA((2, per_step + 1))],
    )
    return pl.pallas_call(
        functools.partial(_attn_kernel, layer),
        grid_spec=grid_spec,
        out_shape=(jax.ShapeDtypeStruct((bsz, seq, width), BF16), jax.ShapeDtypeStruct((n_tok, width), F32)),
        compiler_params=_params(2),
        name="attn",
    )(page_table, qt, kb, vtb, means, q_s, k_s, v_s, ck_t, cv_t)


def _merge_kernel(x_ref, attn_ref, sga_ref, sig_ref, yb_ref, wpa_ref, wo_ref, o_ref):
    gated = attn_ref[...].astype(BF16) * sga_ref[...]
    a = jnp.dot(gated, wpa_ref[...], preferred_element_type=F32)
    y = sig_ref[...].astype(F32) * a + yb_ref[...].astype(F32)
    o_ref[...] = x_ref[...] + jnp.dot(y.astype(BF16), wo_ref[...], preferred_element_type=F32)


def _merge(layer, x, attn, sga, sig, yb, wpa, wo, tm):
    n, d_model = x.shape
    assert n % tm == 0
    tile = lambda width: pl.BlockSpec((tm, width), lambda t: (t, 0))
    return pl.pallas_call(
        _merge_kernel,
        grid=(n // tm,),
        in_specs=[tile(d_model), tile(ATTN_WIDTH), tile(ATTN_WIDTH), tile(d_model), tile(d_model),
                  _layer_slice(wpa, layer), _layer_slice(wo, layer)],
        out_specs=tile(d_model),
        out_shape=jax.ShapeDtypeStruct((n, d_model), F32),
        compiler_params=_params(1),
        name="merge",
    )(x, attn, sga, sig, yb, wpa, wo)


def _proj_sample_kernel(past_len, x_ref, nw_ref, win_ref, ones_ref, cos_ref, sin_ref, qw_ref, kw_ref,
                        wgrp_ref, pscale_ref, wpb_ref, state_ref,
                        q_out, k_out, v_out, kst_out, vst_out, sga_out, sig_out, yb_out, pool_out,
                        slab_scr):
    n_tok, d_model = x_ref.shape
    n_seq = state_ref.shape[1]
    n_new = n_tok // n_seq
    n_slab = slab_scr.shape[0]
    offs = _col_ranges()
    h = _rms_norm_rows(x_ref[...], nw_ref[...]).astype(BF16)

    def proj(lo, hi):
        return jnp.dot(h, win_ref[:, lo:hi], preferred_element_type=F32)

    def to_slabs(val):
        for c in range(n_slab):
            slab_scr[c] = val[:, c * LANES:(c + 1) * LANES]

    def step_rows(s):
        return jnp.concatenate(
            [slab_scr[c, pl.ds(s, n_seq, stride=n_new), :] for c in range(n_slab)], axis=-1)

    ones_bd = ones_ref[...]
    cos_t = cos_ref[...]
    sin_t = sin_ref[...]
    q_out[...] = _head_norm_rope(proj(offs[0], offs[1]), ones_bd, cos_t, sin_t, qw_ref[...]) * ATTN_SCALE

    k = _head_norm_rope(proj(offs[1], offs[2]), ones_bd, cos_t, sin_t, kw_ref[...])
    k_out[...] = k
    to_slabs(k)
    for s in range(n_new):
        kst_out[s] = step_rows(s).T

    v = proj(offs[2], offs[3])
    v_out[...] = v
    to_slabs(v)
    for s in range(n_new):
        vst_out[s] = step_rows(s).T

    sga_out[...] = _silu(proj(offs[3], offs[4])).astype(BF16)

    zu = proj(offs[4], offs[5])
    to_slabs(zu)
    hist = [state_ref[j] for j in range(POOL_BUF)] + [step_rows(s) for s in range(n_new)]
    for j in range(POOL_BUF):
        pool_out[j] = hist[len(hist) - POOL_BUF + j]
    for s in range(n_new):
        cur = POOL_BUF + s
        parts = []
        for g, w in enumerate(POOL_WINDOWS):
            cols = slice(g * POOL_GROUP_WIDTH, (g + 1) * POOL_GROUP_WIDTH)
            acc = hist[cur][:, cols]
            for back in range(1, w):
                acc = acc + hist[cur - back][:, cols]
            cnt = float(min(w, past_len + s + 1))
            parts.append(acc / cnt - hist[cur][:, cols])
        ds = jnp.concatenate(parts, axis=-1)
        for c in range(n_slab):
            slab_scr[c, pl.ds(s, n_seq, stride=n_new), :] = ds[:, c * LANES:(c + 1) * LANES]
    d = jnp.concatenate([slab_scr[c] for c in range(n_slab)], axis=-1)

    zgp = proj(offs[5], offs[6])
    zgb = proj(offs[6] + d_model, offs[6] + 2 * d_model)
    yb_out[...] = _pool_project(d, zgp, zgb, wgrp_ref, pscale_ref, wpb_ref).astype(BF16)
    sig_out[...] = jax.nn.sigmoid(proj(offs[6], offs[6] + d_model)).astype(BF16)


def _proj_sample(layer, x, nw, win, ones_bd, cos_t, sin_t, qw, kw, wgrp, pscale, wpb, state_t, past_len):
    n_tok, d_model = x.shape
    _, n_hist, n_seq, pool_w = state_t.shape
    assert n_hist == POOL_BUF and pool_w == POOL_WIDTH and n_tok % n_seq == 0
    n_new = n_tok // n_seq
    assert ATTN_WIDTH == POOL_WIDTH
    n_slab = POOL_WIDTH // LANES
    full = lambda shape: pl.BlockSpec(shape, lambda t: (0,) * len(shape))
    out_shape = (
        jax.ShapeDtypeStruct((n_tok, ATTN_WIDTH), F32),
        jax.ShapeDtypeStruct((n_tok, ATTN_WIDTH), F32),
        jax.ShapeDtypeStruct((n_tok, ATTN_WIDTH), F32),
        jax.ShapeDtypeStruct((n_new, ATTN_WIDTH, n_seq), F32),
        jax.ShapeDtypeStruct((n_new, ATTN_WIDTH, n_seq), F32),
        jax.ShapeDtypeStruct((n_tok, ATTN_WIDTH), BF16),
        jax.ShapeDtypeStruct((n_tok, d_model), BF16),
        jax.ShapeDtypeStruct((n_tok, d_model), BF16),
        jax.ShapeDtypeStruct((POOL_BUF, n_seq, POOL_WIDTH), F32),
    )
    args = (x, nw, win, ones_bd, cos_t, sin_t, qw, kw, wgrp, pscale, wpb, state_t)
    return pl.pallas_call(
        functools.partial(_proj_sample_kernel, past_len),
        grid=(1,),
        in_specs=[_layer_slice(a, layer) if any(a is w for w in (win, wgrp, wpb, state_t)) else full(a.shape)
                  for a in args],
        out_specs=tuple(full(o.shape) for o in out_shape),
        out_shape=out_shape,
        scratch_shapes=[pltpu.VMEM((n_slab, n_tok, LANES), F32)],
        compiler_params=_params(1),
        name="proj_sample",
    )(*args)


def _rope_tables(pos):
    half = HEAD_DIM // 2
    inv_freq = jnp.exp(-math.log(ROPE_THETA) * jnp.arange(half, dtype=F32) / half)
    ang = pos.astype(F32)[:, None] * inv_freq[None, :]
    cos, sin = jnp.cos(ang), jnp.sin(ang)
    reps = LANES // HEAD_DIM
    return (jnp.tile(jnp.concatenate([cos, cos], axis=-1), (1, reps)),
            jnp.tile(jnp.concatenate([-sin, sin], axis=-1), (1, reps)))


def _norm_rows(w):
    half = HEAD_DIM // 2
    reps = LANES // HEAD_DIM
    swapped = jnp.concatenate([w[half:], w[:half]])
    return jnp.stack([jnp.tile(w, reps), jnp.tile(swapped, reps)]).astype(F32)


def kernel(x_prompt, x_sample, cache_k, cache_v, state_pool, page_table, norm_w, w_in, q_norm_w,
           k_norm_w, w_pool_grp, pool_scale, w_proj_attn, w_proj_pool, w_out):
    bp, sp, d_model = x_prompt.shape
    bs, ss, _ = x_sample.shape
    depth, n_phys, page, n_heads, head_dim = cache_k.shape
    assert (n_heads, head_dim) == (N_HEADS, HEAD_DIM) and page == LANES
    n_pages = page_table.shape[1]
    past_len = n_pages * page

    cos_p, sin_p = _rope_tables(jnp.arange(sp, dtype=jnp.int32))
    cos_s, sin_s = _rope_tables(past_len + jnp.arange(ss, dtype=jnp.int32))
    cos_s, sin_s = jnp.tile(cos_s, (bs, 1)), jnp.tile(sin_s, (bs, 1))
    ones_bd = jnp.kron(jnp.eye(N_HEADS, dtype=F32), jnp.ones((HEAD_DIM, HEAD_DIM), F32)).astype(BF16)

    ck_t = cache_k.transpose(0, 1, 3, 4, 2).reshape(depth, n_phys, ATTN_WIDTH, page)
    cv_t = cache_v.transpose(0, 1, 3, 4, 2).reshape(depth, n_phys, ATTN_WIDTH, page)
    state_t = state_pool.transpose(0, 2, 1, 3)

    w_in_b = w_in.astype(BF16)
    w_grp_b = w_pool_grp.astype(BF16)
    w_pa_b = w_proj_attn.astype(BF16)
    w_pb_b = w_proj_pool.astype(BF16)
    w_o_b = w_out.astype(BF16)

    xp = x_prompt
    xs = x_sample.reshape(bs * ss, d_model)
    kv_pages = []
    pp_l, ks_l, vs_l, ps_l = [], [], [], []
    for l in range(depth):
        nw = norm_w[l][None, :]
        qw, kw = _norm_rows(q_norm_w[l]), _norm_rows(k_norm_w[l])
        pscale = pool_scale[l][None, :]

        qt, kb, kt, means, vtf, vtb, sga, sig, yb, plast = _proj_prompt(
            l, kv_pages if l == depth - 1 else [], xp, nw, w_in_b, ones_bd, cos_p, sin_p, qw, kw,
            w_grp_b, pscale, w_pb_b)
        kv_pages += [kt, vtf]
        pp_l.append(plast)
        q_s, k_s, v_s, kst, vst, sga_s, sig_s, yb_s, pool_s = _proj_sample(
            l, xs, nw, w_in_b, ones_bd, cos_s, sin_s, qw, kw, w_grp_b, pscale, w_pb_b, state_t, past_len)

        attn, attn_s = _attn(l, page_table, qt, kb, vtb, means.reshape(bp, -1, ATTN_WIDTH),
                             q_s, k_s, v_s, ck_t, cv_t)

        flat = lambda a: a.reshape(bp * sp, a.shape[-1])
        xp = _merge(l, flat(xp), flat(attn), flat(sga), flat(sig), flat(yb), w_pa_b, w_o_b,
                    MERGE_TILE).reshape(bp, sp, d_model)
        xs = _merge(l, xs, attn_s, sga_s, sig_s, yb_s, w_pa_b, w_o_b, bs * ss)
        to_steps = lambda a: a.reshape(ss, N_HEADS, HEAD_DIM, bs).transpose(3, 0, 1, 2)
        ks_l.append(to_steps(kst))
        vs_l.append(to_steps(vst))
        ps_l.append(pool_s.transpose(1, 0, 2))

    to_pages = lambda a: a.reshape(depth, bp, sp // page, N_HEADS, HEAD_DIM, page).transpose(0, 1, 2, 5, 3, 4)
    return (xp, xs.reshape(bs, ss, d_model), to_pages(kv_pages[-2]), to_pages(kv_pages[-1]), jnp.stack(pp_l),
            jnp.stack(ks_l), jnp.stack(vs_l), jnp.stack(ps_l))
```

```python
import functools
import math

import jax
import jax.numpy as jnp
from jax import lax
from jax.experimental import pallas as pl
from jax.experimental.pallas import tpu as pltpu

F32 = jnp.float32
BF16 = jnp.bfloat16

N_HEADS = 8
HEAD_DIM = 64
ATTN_WIDTH = N_HEADS * HEAD_DIM
MOBA_BLOCK = 256
MOBA_TOPK = 3
POOL_WINDOWS = (2, 4, 8, 16)
POOL_GROUP_WIDTH = 128
POOL_WIDTH = len(POOL_WINDOWS) * POOL_GROUP_WIDTH
POOL_BUF = max(POOL_WINDOWS) - 1
POOL_HALO = 16
ROPE_THETA = 10000.0
RMS_EPS = 1e-6
NEG_INF = -1e30
ATTN_SCALE = HEAD_DIM ** -0.5
LOG2_E = math.log2(math.e)
BF16_SUBLANES = 16
ACC_ROWS = HEAD_DIM + BF16_SUBLANES

LANES = 128
VMEM_LIMIT_BYTES = 56 * 1024 * 1024

PROJ_TILE = 2 * MOBA_BLOCK
MERGE_TILE = 4 * MOBA_BLOCK
KV_UNROLL = 8


def _resident(shape):
    return pl.BlockSpec(shape, lambda *_: (0,) * len(shape), pipeline_mode=pl.Buffered(1))


def _layer_slice(stacked, layer):
    rest = stacked.shape[1:]
    return pl.BlockSpec((None,) + rest, lambda *_: (layer,) + (0,) * len(rest), pipeline_mode=pl.Buffered(1))


def _params(n_axes):
    return pltpu.CompilerParams(dimension_semantics=("arbitrary",) * n_axes,
                                vmem_limit_bytes=VMEM_LIMIT_BYTES)


def _silu(z):
    return z * jax.nn.sigmoid(z)


def _rms_norm_rows(x, w_row):
    ms = jnp.mean(x * x, axis=-1, keepdims=True)
    return x * lax.rsqrt(ms + RMS_EPS) * w_row


def _head_norm_rope(z, ones_bd, cos_t, sin_t, w_rows):
    m = z.shape[0]
    ssq = jnp.dot((z * z).astype(BF16), ones_bd, preferred_element_type=F32)
    r = lax.rsqrt(ssq * (1.0 / HEAD_DIM) + RMS_EPS)
    cw = cos_t * w_rows[0:1, :]
    sw = sin_t * w_rows[1:2, :]
    lane = lax.broadcasted_iota(jnp.int32, (m, LANES), 1)
    first_half = (lane % HEAD_DIM) < (HEAD_DIM // 2)
    outs = []
    for c in range(ATTN_WIDTH // LANES):
        zc = z[:, c * LANES:(c + 1) * LANES]
        partner = jnp.where(first_half,
                            pltpu.roll(zc, LANES - HEAD_DIM // 2, 1),
                            pltpu.roll(zc, HEAD_DIM // 2, 1))
        outs.append((zc * cw + partner * sw) * r[:, c * LANES:(c + 1) * LANES])
    return jnp.concatenate(outs, axis=-1)


def _col_ranges():
    sizes = (ATTN_WIDTH,) * 4 + (POOL_WIDTH,) * 2
    offs = [0]
    for s in sizes:
        offs.append(offs[-1] + s)
    return offs


def _pool_project(d, zgp, zgb, wgrp_ref, pscale_ref, wpb_ref):
    parts = []
    for g in range(len(POOL_WINDOWS)):
        dg = d[:, g * POOL_GROUP_WIDTH:(g + 1) * POOL_GROUP_WIDTH].astype(BF16)
        parts.append(jnp.dot(dg, wgrp_ref[g], preferred_element_type=F32))
    pool = jnp.concatenate(parts, axis=-1) * pscale_ref[...]
    pg = (pool * _silu(zgp)).astype(BF16)
    b = jnp.dot(pg, wpb_ref[...], preferred_element_type=F32)
    return jax.nn.sigmoid(zgb) * b


N_PROJ_PROMPT_OUT = 10


def _proj_prompt_kernel(x_ref, nw_ref, win_ref, ones_ref, cos_ref, sin_ref, qw_ref, kw_ref,
                        wgrp_ref, pscale_ref, wpb_ref, *rest):
    (qt_out, kb_out, kt_out, mean_out, vtf_out, vtb_out,
     sga_out, sig_out, yb_out, plast_out, ubuf) = rest[-(N_PROJ_PROMPT_OUT + 1):]
    earlier = rest[:-(N_PROJ_PROMPT_OUT + 1)]
    own = len(earlier) // 2
    for slot in range(own):
        kt_out[slot] = earlier[2 * slot][0]
        vtf_out[slot] = earlier[2 * slot + 1][0]
    t = pl.program_id(1)
    tm = x_ref.shape[1]
    d_model = x_ref.shape[2]
    offs = _col_ranges()

    @pl.when((pl.program_id(0) == 0) & (t == 0))
    def _():
        ubuf[0:POOL_HALO, :] = jnp.zeros((POOL_HALO, POOL_WIDTH), F32)

    h = _rms_norm_rows(x_ref[0], nw_ref[...]).astype(BF16)

    def proj(lo, hi):
        return jnp.dot(h, win_ref[:, lo:hi], preferred_element_type=F32)

    ones_bd = ones_ref[...]
    cos_t = cos_ref[...]
    sin_t = sin_ref[...]
    n_blk = tm // MOBA_BLOCK
    n_page = tm // LANES

    q = _head_norm_rope(proj(offs[0], offs[1]), ones_bd, cos_t, sin_t, qw_ref[...]) * (ATTN_SCALE * LOG2_E)
    qt = q.T.astype(BF16)
    for i in range(n_blk):
        qt_out[0, i] = qt[:, i * MOBA_BLOCK:(i + 1) * MOBA_BLOCK]

    k = _head_norm_rope(proj(offs[1], offs[2]), ones_bd, cos_t, sin_t, kw_ref[...])
    kb_out[0] = k.astype(BF16)
    kt = k.T
    for i in range(n_page):
        kt_out[own, 0, i] = kt[:, i * LANES:(i + 1) * LANES]
    for i in range(n_blk):
        mean_out[0, i] = jnp.mean(k[i * MOBA_BLOCK:(i + 1) * MOBA_BLOCK], axis=0, keepdims=True)

    vt = proj(offs[2], offs[3]).T
    for i in range(n_page):
        vtf_out[own, 0, i] = vt[:, i * LANES:(i + 1) * LANES]
    vtb = vt.astype(BF16)
    for i in range(n_blk):
        vtb_out[0, i] = vtb[:, i * MOBA_BLOCK:(i + 1) * MOBA_BLOCK]

    sga_out[0] = _silu(proj(offs[3], offs[4])).astype(BF16)

    zu = proj(offs[4], offs[5])
    ubuf[POOL_HALO:POOL_HALO + tm, :] = zu
    pos = t * tm + lax.broadcasted_iota(jnp.int32, (tm, 1), 0)
    parts = []
    for g, w in enumerate(POOL_WINDOWS):
        cols = slice(g * POOL_GROUP_WIDTH, (g + 1) * POOL_GROUP_WIDTH)
        zug = zu[:, cols]
        acc = zug
        for back in range(1, w):
            acc = acc + ubuf[pl.ds(POOL_HALO - back, tm), cols]
        cnt = jnp.minimum(w, pos + 1).astype(F32)
        parts.append(acc / cnt - zug)
    d = jnp.concatenate(parts, axis=-1)
    last = t == pl.num_programs(1) - 1
    ubuf[0:POOL_HALO, :] = jnp.where(last, 0.0, zu[tm - POOL_HALO:tm, :])
    plast_out[0] = zu[tm - POOL_BUF:tm, :]

    zgp = proj(offs[5], offs[6])
    zgb = proj(offs[6] + d_model, offs[6] + 2 * d_model)
    yb_out[0] = _pool_project(d, zgp, zgb, wgrp_ref, pscale_ref, wpb_ref).astype(BF16)
    sig_out[0] = jax.nn.sigmoid(proj(offs[6], offs[6] + d_model)).astype(BF16)


def _proj_prompt(layer, earlier_pages, x, nw, win, ones_bd, cos_t, sin_t, qw, kw, wgrp, pscale, wpb):
    bsz, seq, d_model = x.shape
    tm = PROJ_TILE
    assert seq % tm == 0 and tm % MOBA_BLOCK == 0
    n_t = seq // tm
    n_blk, n_page = seq // MOBA_BLOCK, seq // LANES
    n_slots = len(earlier_pages) // 2 + 1
    tile = lambda width: pl.BlockSpec((1, tm, width), lambda b, t: (b, t, 0))
    paged = lambda per, minor: pl.BlockSpec((1, per, ATTN_WIDTH, minor), lambda b, t: (b, t, 0, 0))
    pages = lambda slots: pl.BlockSpec((slots, 1, tm // LANES, ATTN_WIDTH, LANES), lambda b, t: (0, b, t, 0, 0))
    out_shape = (
        jax.ShapeDtypeStruct((bsz, n_blk, ATTN_WIDTH, MOBA_BLOCK), BF16),
        jax.ShapeDtypeStruct((bsz, seq, ATTN_WIDTH), BF16),
        jax.ShapeDtypeStruct((n_slots, bsz, n_page, ATTN_WIDTH, LANES), F32),
        jax.ShapeDtypeStruct((bsz, n_blk, 1, ATTN_WIDTH), F32),
        jax.ShapeDtypeStruct((n_slots, bsz, n_page, ATTN_WIDTH, LANES), F32),
        jax.ShapeDtypeStruct((bsz, n_blk, ATTN_WIDTH, MOBA_BLOCK), BF16),
        jax.ShapeDtypeStruct((bsz, seq, ATTN_WIDTH), BF16),
        jax.ShapeDtypeStruct((bsz, seq, d_model), BF16),
        jax.ShapeDtypeStruct((bsz, seq, d_model), BF16),
        jax.ShapeDtypeStruct((bsz, POOL_BUF, POOL_WIDTH), F32),
    )
    out_specs = (
        paged(tm // MOBA_BLOCK, MOBA_BLOCK),
        tile(ATTN_WIDTH),
        pages(n_slots),
        pl.BlockSpec((1, tm // MOBA_BLOCK, 1, ATTN_WIDTH), lambda b, t: (b, t, 0, 0)),
        pages(n_slots),
        paged(tm // MOBA_BLOCK, MOBA_BLOCK),
        tile(ATTN_WIDTH),
        tile(d_model),
        tile(d_model),
        pl.BlockSpec((1, POOL_BUF, POOL_WIDTH), lambda b, t: (b, 0, 0)),
    )
    in_specs = [
        tile(d_model),
        _resident(nw.shape), _layer_slice(win, layer), _resident(ones_bd.shape),
        pl.BlockSpec((tm, LANES), lambda b, t: (t, 0)),
        pl.BlockSpec((tm, LANES), lambda b, t: (t, 0)),
        _resident(qw.shape), _resident(kw.shape),
        _layer_slice(wgrp, layer), _resident(pscale.shape), _layer_slice(wpb, layer),
    ]
    in_specs += [pages(1)] * len(earlier_pages)
    args = [x, nw, win, ones_bd, cos_t, sin_t, qw, kw, wgrp, pscale, wpb, *earlier_pages]
    return pl.pallas_call(
        _proj_prompt_kernel,
        grid=(bsz, n_t),
        in_specs=in_specs,
        out_specs=out_specs,
        out_shape=out_shape,
        scratch_shapes=[pltpu.VMEM((POOL_HALO + tm, POOL_WIDTH), F32)],
        compiler_params=_params(2),
        name="proj_prompt",
    )(*args)


def _select_bias(sc, n_valid, n_blk):
    jrow = lax.broadcasted_iota(jnp.int32, sc.shape, 0)
    jrow_f = jrow.astype(F32)
    valid = jrow < n_valid
    s = jnp.where(valid, sc, -jnp.inf)
    for _ in range(MOBA_TOPK):
        top = jnp.max(s, axis=0, keepdims=True)
        first = jnp.min(jnp.where(s == top, jrow_f, float(n_blk)), axis=0, keepdims=True)
        s = jnp.where(jrow_f == first, -jnp.inf, s)
    return jnp.where(valid & (s == -jnp.inf), 0.0, NEG_INF)


def _attn_prompt_tile(i, qt_ref, kb_ref, vtb_ref, mean_ref, o_ref, qaug_scr, sa_scr, sb_scr, m_scr, acc_scr,
                      before_loops):
    n_blk = mean_ref.shape[1]
    blk = MOBA_BLOCK
    pair_w = 2 * HEAD_DIM
    qt = qt_ref[0, 0]

    means = mean_ref[0].astype(BF16)
    tiled = jnp.concatenate([means] * N_HEADS, axis=0)
    row_head = lax.broadcasted_iota(jnp.int32, tiled.shape, 0) // n_blk
    col_head = lax.broadcasted_iota(jnp.int32, tiled.shape, 1) // HEAD_DIM
    means_bd = jnp.where(row_head == col_head, tiled, jnp.zeros_like(tiled))
    sc = jnp.dot(means_bd, qt, preferred_element_type=F32)

    pair_row = lax.broadcasted_iota(jnp.int32, (pair_w, blk), 0)
    blk_row = lax.broadcasted_iota(jnp.int32, (n_blk, blk), 0)
    tail_row = lax.broadcasted_iota(jnp.int32, (pair_w - n_blk, blk), 0)
    tail = jnp.where(tail_row == 0, NEG_INF, 0.0)
    for h in range(N_HEADS):
        bias = _select_bias(sc[h * n_blk:(h + 1) * n_blk], i, n_blk)
        bias = jnp.where(blk_row == i, 0.0, bias)
        qpair = qt[(h // 2) * pair_w:(h // 2 + 1) * pair_w, :]
        mine = (pair_row < HEAD_DIM) if h % 2 == 0 else (pair_row >= HEAD_DIM)
        qh = jnp.where(mine, qpair, jnp.zeros_like(qpair))
        qaug_scr[h] = jnp.concatenate([qh, jnp.concatenate([bias, tail], axis=0).astype(BF16)], axis=0)

    lane_blk = lax.broadcasted_iota(jnp.int32, (blk, pair_w), 1)

    def keys_aug(j, bias_row):
        start = pl.multiple_of(j * blk, blk)
        onehot = jnp.where(lane_blk == bias_row, 1.0, 0.0).astype(BF16)
        return [jnp.concatenate([kb_ref[0, pl.ds(start, blk), p * pair_w:(p + 1) * pair_w], onehot], axis=1)
                for p in range(N_HEADS // 2)]

    def scores(kj, h):
        return jnp.dot(kj[h // 2], qaug_scr[h], preferred_element_type=F32)

    ones_rows = jnp.ones((ACC_ROWS - HEAD_DIM, blk), BF16)

    def pv_and_sum(h, v_blk, p):
        vt_h = vtb_ref[0, v_blk, h * HEAD_DIM:(h + 1) * HEAD_DIM, :]
        return jnp.dot(jnp.concatenate([vt_h, ones_rows], axis=0), p.astype(BF16), preferred_element_type=F32)

    def first_block(h, st, v_blk):
        m0 = jnp.max(st, axis=0, keepdims=True)
        m_scr[h] = m0
        acc_scr[h] = pv_and_sum(h, v_blk, jnp.exp2(st - m0))

    def next_block(h, sj, v_blk):
        m_old = m_scr[h]
        m_new = jnp.maximum(m_old, jnp.max(sj, axis=0, keepdims=True))
        acc_scr[h] = jnp.exp2(m_old - m_new) * acc_scr[h] + pv_and_sum(h, v_blk, jnp.exp2(sj - m_new))
        m_scr[h] = m_new

    key_i = lax.broadcasted_iota(jnp.int32, (blk, blk), 0)
    qry_i = lax.broadcasted_iota(jnp.int32, (blk, blk), 1)
    causal = key_i <= qry_i
    ka = keys_aug(i, i)
    for h in range(N_HEADS):
        sb_scr[h] = scores(ka, h)
    k0 = keys_aug(0, 0)
    for h in range(N_HEADS):
        sa_scr[h] = scores(k0, h)
        first_block(h, jnp.where(causal, sb_scr[h], NEG_INF), i)

    def make_body(unroll):
        def body(t, base):
            for u in range(unroll):
                cur = base + t * unroll + u
                nxt = cur + 1
                nxt_keys = jnp.minimum(nxt, i - 1)
                k_aug = keys_aug(nxt_keys, jnp.where(nxt < i, nxt, n_blk))
                read, write = (sa_scr, sb_scr) if u % 2 == 0 else (sb_scr, sa_scr)
                for h in range(N_HEADS):
                    write[h] = scores(k_aug, h)
                    next_block(h, read[h], jnp.minimum(cur, i - 1))
            return base
        return body

    before_loops()
    n_wide = i // KV_UNROLL
    lax.fori_loop(0, n_wide, make_body(KV_UNROLL), 0)
    done = n_wide * KV_UNROLL
    lax.fori_loop(0, (i - done + 1) // 2, make_body(2), done)

    outs = []
    for h in range(N_HEADS):
        acc = acc_scr[h]
        outs.append(acc[0:HEAD_DIM] / acc[HEAD_DIM:HEAD_DIM + 1])
    o_ref[0] = jnp.concatenate(outs, axis=0).T.astype(BF16)


def _attn_sample_seq(q, k_new, v_new, kt, vt):
    n_new = q.shape[0]
    n_full = kt.shape[1] // MOBA_BLOCK
    tiled = jnp.concatenate([q] * N_HEADS, axis=0)
    row_head = lax.broadcasted_iota(jnp.int32, tiled.shape, 0) // n_new
    col_head = lax.broadcasted_iota(jnp.int32, tiled.shape, 1) // HEAD_DIM
    head_lanes = row_head == col_head
    q_bd = jnp.where(head_lanes, tiled, 0.0).astype(BF16)

    s_past = jnp.dot(q_bd, kt, preferred_element_type=F32)
    nt_dims = (((1,), (1,)), ((), ()))
    s_new = lax.dot_general(q_bd, k_new.astype(BF16), nt_dims, preferred_element_type=F32)
    row_step = lax.broadcasted_iota(jnp.int32, s_new.shape, 0) % n_new
    col_step = lax.broadcasted_iota(jnp.int32, s_new.shape, 1)
    s_new = jnp.where(col_step <= row_step, s_new, NEG_INF)

    blocks = [s_past[:, n * MOBA_BLOCK:(n + 1) * MOBA_BLOCK] for n in range(n_full)]
    score = [jnp.sum(sb, axis=1, keepdims=True) for sb in blocks]
    n_sel = min(MOBA_TOPK, n_full)
    lane = lax.broadcasted_iota(jnp.int32, (tiled.shape[0], LANES), 1)
    by_lane = jnp.full(lane.shape, -jnp.inf, F32)
    for n in range(n_full):
        by_lane = jnp.where(lane == n, score[n], by_lane)
    masked = []
    for n in range(n_full):
        ahead = (by_lane > score[n]) | ((lane < n) & (by_lane == score[n]))
        rank = jnp.sum(jnp.where(ahead, 1.0, 0.0), axis=1, keepdims=True)
        masked.append(blocks[n] + jnp.where(rank < n_sel, 0.0, NEG_INF))

    m = jnp.max(s_new, axis=1, keepdims=True)
    for sb in masked:
        m = jnp.maximum(m, jnp.max(sb, axis=1, keepdims=True))
    p_new = jnp.exp(s_new - m)
    l = jnp.sum(p_new, axis=1, keepdims=True)
    probs = []
    for sb in masked:
        pb = jnp.exp(sb - m)
        l = l + jnp.sum(pb, axis=1, keepdims=True)
        probs.append(pb.astype(BF16))
    p_past = jnp.concatenate(probs, axis=1)
    out = lax.dot_general(p_past, vt, nt_dims, preferred_element_type=F32)
    out = out + jnp.dot(p_new.astype(BF16), v_new.astype(BF16), preferred_element_type=F32)
    out = jnp.where(head_lanes, out / l, 0.0)
    res = out[0:n_new]
    for hh in range(1, N_HEADS):
        res = res + out[hh * n_new:(hh + 1) * n_new]
    return res


def _attn_kernel(layer, pt_ref, qt_ref, kb_ref, vtb_ref, mean_ref, qs_ref, ks_ref, vs_ref, ck_hbm, cv_hbm,
                 o_ref, os_ref, qaug_scr, sa_scr, sb_scr, m_scr, acc_scr, kbuf, vbuf, sem):
    step = pl.program_id(0) * pl.num_programs(1) + pl.program_id(1)
    n_slots = kbuf.shape[0]
    per_step = n_slots - 1
    n_seq = pt_ref.shape[0]
    n_pages = pt_ref.shape[1]
    page = ck_hbm.shape[3]
    n_new = qs_ref.shape[0] // per_step
    first = step * per_step

    def page_copies(seq):
        slot = lax.rem(seq, n_slots)
        copies = []
        for pg in range(n_pages):
            phys = pt_ref[seq, pg]
            win = pl.ds(pg * page, page)
            copies.append(pltpu.make_async_copy(ck_hbm.at[layer, phys], kbuf.at[slot, :, win], sem.at[0, slot]))
            copies.append(pltpu.make_async_copy(cv_hbm.at[layer, phys], vbuf.at[slot, :, win], sem.at[1, slot]))
        return copies

    def start_pages(seq):
        @pl.when(seq < n_seq)
        def _():
            for c in page_copies(seq):
                c.start()

    @pl.when(step == 0)
    def _():
        for j in range(per_step):
            for c in page_copies(j):
                c.start()

    start_pages(first + per_step)
    for j in range(per_step):
        for c in page_copies(first + j):
            c.wait()
    for j in range(per_step):
        rows = slice(j * n_new, (j + 1) * n_new)
        slot = lax.rem(first + j, n_slots)
        os_ref[rows, :] = _attn_sample_seq(qs_ref[rows, :], ks_ref[rows, :], vs_ref[rows, :],
                                           kbuf[slot].astype(BF16), vbuf[slot].astype(BF16))

    def start_rest_of_next_step():
        for j in range(1, per_step):
            start_pages(first + per_step + j)

    _attn_prompt_tile(pl.program_id(1), qt_ref, kb_ref, vtb_ref, mean_ref, o_ref,
                      qaug_scr, sa_scr, sb_scr, m_scr, acc_scr, start_rest_of_next_step)


def _attn(layer, page_table, qt, kb, vtb, means, q_s, k_s, v_s, ck_t, cv_t):
    bsz, n_blk, width, blk = qt.shape
    seq = kb.shape[1]
    assert n_blk < 2 * HEAD_DIM
    n_seq, n_pages = page_table.shape
    n_tok = q_s.shape[0]
    n_new = n_tok // n_seq
    page = ck_t.shape[3]
    past = n_pages * page
    assert past % MOBA_BLOCK == 0 and past >= MOBA_BLOCK and ck_t.shape[2] == width
    n_steps = bsz * n_blk
    assert n_seq % n_steps == 0
    per_step = n_seq // n_steps
    sample_tile = pl.BlockSpec((per_step * n_new, width), lambda b, i, pt: (b * n_blk + i, 0))
    grid_spec = pltpu.PrefetchScalarGridSpec(
        num_scalar_prefetch=1,
        grid=(bsz, n_blk),
        in_specs=[
            pl.BlockSpec((1, 1, width, blk), lambda b, i, pt: (b, i, 0, 0)),
            pl.BlockSpec((1, seq, width), lambda b, i, pt: (b, 0, 0), pipeline_mode=pl.Buffered(1)),
            pl.BlockSpec((1, n_blk, width, blk), lambda b, i, pt: (b, 0, 0, 0), pipeline_mode=pl.Buffered(1)),
            pl.BlockSpec((1, n_blk, width), lambda b, i, pt: (b, 0, 0)),
            sample_tile, sample_tile, sample_tile,
            pl.BlockSpec(memory_space=pl.ANY), pl.BlockSpec(memory_space=pl.ANY),
        ],
        out_specs=(pl.BlockSpec((1, blk, width), lambda b, i, pt: (b, i, 0)), sample_tile),
        scratch_shapes=[pltpu.VMEM((N_HEADS, 4 * HEAD_DIM, blk), BF16),
                        pltpu.VMEM((N_HEADS, blk, blk), F32),
                        pltpu.VMEM((N_HEADS, blk, blk), F32),
                        pltpu.VMEM((N_HEADS, 1, blk), F32),
                        pltpu.VMEM((N_HEADS, ACC_ROWS, blk), F32),
                        pltpu.VMEM((per_step + 1, width, past), F32),
                        pltpu.VMEM((per_step + 1, width, past), F32),
                        pltpu.SemaphoreType.DMA((2, per_step + 1))],
    )
    return pl.pallas_call(
        functools.partial(_attn_kernel, layer),
        grid_spec=grid_spec,
        out_shape=(jax.ShapeDtypeStruct((bsz, seq, width), BF16), jax.ShapeDtypeStruct((n_tok, width), F32)),
        compiler_params=_params(2),
        name="attn",
    )(page_table, qt, kb, vtb, means, q_s, k_s, v_s, ck_t, cv_t)


def _merge_kernel(x_ref, attn_ref, sga_ref, sig_ref, yb_ref, wpa_ref, wo_ref, o_ref):
    gated = attn_ref[...].astype(BF16) * sga_ref[...]
    a = jnp.dot(gated, wpa_ref[...], preferred_element_type=F32)
    y = sig_ref[...].astype(F32) * a + yb_ref[...].astype(F32)
    o_ref[...] = x_ref[...] + jnp.dot(y.astype(BF16), wo_ref[...], preferred_element_type=F32)


def _merge(layer, x, attn, sga, sig, yb, wpa, wo, tm):
    n, d_model = x.shape
    assert n % tm == 0
    tile = lambda width: pl.BlockSpec((tm, width), lambda t: (t, 0))
    return pl.pallas_call(
        _merge_kernel,
        grid=(n // tm,),
        in_specs=[tile(d_model), tile(ATTN_WIDTH), tile(ATTN_WIDTH), tile(d_model), tile(d_model),
                  _layer_slice(wpa, layer), _layer_slice(wo, layer)],
        out_specs=tile(d_model),
        out_shape=jax.ShapeDtypeStruct((n, d_model), F32),
        compiler_params=_params(1),
        name="merge",
    )(x, attn, sga, sig, yb, wpa, wo)


def _proj_sample_kernel(past_len, x_ref, nw_ref, win_ref, ones_ref, cos_ref, sin_ref, qw_ref, kw_ref,
                        wgrp_ref, pscale_ref, wpb_ref, state_ref,
                        q_out, k_out, v_out, kst_out, vst_out, sga_out, sig_out, yb_out, pool_out,
                        slab_scr):
    n_tok, d_model = x_ref.shape
    n_seq = state_ref.shape[1]
    n_new = n_tok // n_seq
    n_slab = slab_scr.shape[0]
    offs = _col_ranges()
    h = _rms_norm_rows(x_ref[...], nw_ref[...]).astype(BF16)

    def proj(lo, hi):
        return jnp.dot(h, win_ref[:, lo:hi], preferred_element_type=F32)

    def to_slabs(val):
        for c in range(n_slab):
            slab_scr[c] = val[:, c * LANES:(c + 1) * LANES]

    def step_rows(s):
        return jnp.concatenate(
            [slab_scr[c, pl.ds(s, n_seq, stride=n_new), :] for c in range(n_slab)], axis=-1)

    ones_bd = ones_ref[...]
    cos_t = cos_ref[...]
    sin_t = sin_ref[...]
    q_out[...] = _head_norm_rope(proj(offs[0], offs[1]), ones_bd, cos_t, sin_t, qw_ref[...]) * ATTN_SCALE

    k = _head_norm_rope(proj(offs[1], offs[2]), ones_bd, cos_t, sin_t, kw_ref[...])
    k_out[...] = k
    to_slabs(k)
    for s in range(n_new):
        kst_out[s] = step_rows(s).T

    v = proj(offs[2], offs[3])
    v_out[...] = v
    to_slabs(v)
    for s in range(n_new):
        vst_out[s] = step_rows(s).T

    sga_out[...] = _silu(proj(offs[3], offs[4])).astype(BF16)

    zu = proj(offs[4], offs[5])
    to_slabs(zu)
    hist = [state_ref[j] for j in range(POOL_BUF)] + [step_rows(s) for s in range(n_new)]
    for j in range(POOL_BUF):
        pool_out[j] = hist[len(hist) - POOL_BUF + j]
    for s in range(n_new):
        cur = POOL_BUF + s
        parts = []
        for g, w in enumerate(POOL_WINDOWS):
            cols = slice(g * POOL_GROUP_WIDTH, (g + 1) * POOL_GROUP_WIDTH)
            acc = hist[cur][:, cols]
            for back in range(1, w):
                acc = acc + hist[cur - back][:, cols]
            cnt = float(min(w, past_len + s + 1))
            parts.append(acc / cnt - hist[cur][:, cols])
        ds = jnp.concatenate(parts, axis=-1)
        for c in range(n_slab):
            slab_scr[c, pl.ds(s, n_seq, stride=n_new), :] = ds[:, c * LANES:(c + 1) * LANES]
    d = jnp.concatenate([slab_scr[c] for c in range(n_slab)], axis=-1)

    zgp = proj(offs[5], offs[6])
    zgb = proj(offs[6] + d_model, offs[6] + 2 * d_model)
    yb_out[...] = _pool_project(d, zgp, zgb, wgrp_ref, pscale_ref, wpb_ref).astype(BF16)
    sig_out[...] = jax.nn.sigmoid(proj(offs[6], offs[6] + d_model)).astype(BF16)


def _proj_sample(layer, x, nw, win, ones_bd, cos_t, sin_t, qw, kw, wgrp, pscale, wpb, state_t, past_len):
    n_tok, d_model = x.shape
    _, n_hist, n_seq, pool_w = state_t.shape
    assert n_hist == POOL_BUF and pool_w == POOL_WIDTH and n_tok % n_seq == 0
    n_new = n_tok // n_seq
    assert ATTN_WIDTH == POOL_WIDTH
    n_slab = POOL_WIDTH // LANES
    full = lambda shape: pl.BlockSpec(shape, lambda t: (0,) * len(shape))
    out_shape = (
        jax.ShapeDtypeStruct((n_tok, ATTN_WIDTH), F32),
        jax.ShapeDtypeStruct((n_tok, ATTN_WIDTH), F32),
        jax.ShapeDtypeStruct((n_tok, ATTN_WIDTH), F32),
        jax.ShapeDtypeStruct((n_new, ATTN_WIDTH, n_seq), F32),
        jax.ShapeDtypeStruct((n_new, ATTN_WIDTH, n_seq), F32),
        jax.ShapeDtypeStruct((n_tok, ATTN_WIDTH), BF16),
        jax.ShapeDtypeStruct((n_tok, d_model), BF16),
        jax.ShapeDtypeStruct((n_tok, d_model), BF16),
        jax.ShapeDtypeStruct((POOL_BUF, n_seq, POOL_WIDTH), F32),
    )
    args = (x, nw, win, ones_bd, cos_t, sin_t, qw, kw, wgrp, pscale, wpb, state_t)
    return pl.pallas_call(
        functools.partial(_proj_sample_kernel, past_len),
        grid=(1,),
        in_specs=[_layer_slice(a, layer) if any(a is w for w in (win, wgrp, wpb, state_t)) else full(a.shape)
                  for a in args],
        out_specs=tuple(full(o.shape) for o in out_shape),
        out_shape=out_shape,
        scratch_shapes=[pltpu.VMEM((n_slab, n_tok, LANES), F32)],
        compiler_params=_params(1),
        name="proj_sample",
    )(*args)


def _rope_tables(pos):
    half = HEAD_DIM // 2
    inv_freq = jnp.exp(-math.log(ROPE_THETA) * jnp.arange(half, dtype=F32) / half)
    ang = pos.astype(F32)[:, None] * inv_freq[None, :]
    cos, sin = jnp.cos(ang), jnp.sin(ang)
    reps = LANES // HEAD_DIM
    return (jnp.tile(jnp.concatenate([cos, cos], axis=-1), (1, reps)),
            jnp.tile(jnp.concatenate([-sin, sin], axis=-1), (1, reps)))


def _norm_rows(w):
    half = HEAD_DIM // 2
    reps = LANES // HEAD_DIM
    swapped = jnp.concatenate([w[half:], w[:half]])
    return jnp.stack([jnp.tile(w, reps), jnp.tile(swapped, reps)]).astype(F32)


def kernel(x_prompt, x_sample, cache_k, cache_v, state_pool, page_table, norm_w, w_in, q_norm_w,
           k_norm_w, w_pool_grp, pool_scale, w_proj_attn, w_proj_pool, w_out):
    bp, sp, d_model = x_prompt.shape
    bs, ss, _ = x_sample.shape
    depth, n_phys, page, n_heads, head_dim = cache_k.shape
    assert (n_heads, head_dim) == (N_HEADS, HEAD_DIM) and page == LANES
    n_pages = page_table.shape[1]
    past_len = n_pages * page

    cos_p, sin_p = _rope_tables(jnp.arange(sp, dtype=jnp.int32))
    cos_s, sin_s = _rope_tables(past_len + jnp.arange(ss, dtype=jnp.int32))
    cos_s, sin_s = jnp.tile(cos_s, (bs, 1)), jnp.tile(sin_s, (bs, 1))
    ones_bd = jnp.kron(jnp.eye(N_HEADS, dtype=F32), jnp.ones((HEAD_DIM, HEAD_DIM), F32)).astype(BF16)

    ck_t = cache_k.transpose(0, 1, 3, 4, 2).reshape(depth, n_phys, ATTN_WIDTH, page)
    cv_t = cache_v.transpose(0, 1, 3, 4, 2).reshape(depth, n_phys, ATTN_WIDTH, page)
    state_t = state_pool.transpose(0, 2, 1, 3)

    w_in_b = w_in.astype(BF16)
    w_grp_b = w_pool_grp.astype(BF16)
    w_pa_b = w_proj_attn.astype(BF16)
    w_pb_b = w_proj_pool.astype(BF16)
    w_o_b = w_out.astype(BF16)

    xp = x_prompt
    xs = x_sample.reshape(bs * ss, d_model)
    kv_pages = []
    pp_l, ks_l, vs_l, ps_l = [], [], [], []
    for l in range(depth):
        nw = norm_w[l][None, :]
        qw, kw = _norm_rows(q_norm_w[l]), _norm_rows(k_norm_w[l])
        pscale = pool_scale[l][None, :]

        qt, kb, kt, means, vtf, vtb, sga, sig, yb, plast = _proj_prompt(
            l, kv_pages if l == depth - 1 else [], xp, nw, w_in_b, ones_bd, cos_p, sin_p, qw, kw,
            w_grp_b, pscale, w_pb_b)
        kv_pages += [kt, vtf]
        pp_l.append(plast)
        q_s, k_s, v_s, kst, vst, sga_s, sig_s, yb_s, pool_s = _proj_sample(
            l, xs, nw, w_in_b, ones_bd, cos_s, sin_s, qw, kw, w_grp_b, pscale, w_pb_b, state_t, past_len)

        attn, attn_s = _attn(l, page_table, qt, kb, vtb, means.reshape(bp, -1, ATTN_WIDTH),
                             q_s, k_s, v_s, ck_t, cv_t)

        flat = lambda a: a.reshape(bp * sp, a.shape[-1])
        xp = _merge(l, flat(xp), flat(attn), flat(sga), flat(sig), flat(yb), w_pa_b, w_o_b,
                    MERGE_TILE).reshape(bp, sp, d_model)
        xs = _merge(l, xs, attn_s, sga_s, sig_s, yb_s, w_pa_b, w_o_b, bs * ss)
        to_steps = lambda a: a.reshape(ss, N_HEADS, HEAD_DIM, bs).transpose(3, 0, 1, 2)
        ks_l.append(to_steps(kst))
        vs_l.append(to_steps(vst))
        ps_l.append(pool_s.transpose(1, 0, 2))

    to_pages = lambda a: a.reshape(depth, bp, sp // page, N_HEADS, HEAD_DIM, page).transpose(0, 1, 2, 5, 3, 4)
    return (xp, xs.reshape(bs, ss, d_model), to_pages(kv_pages[-2]), to_pages(kv_pages[-1]), jnp.stack(pp_l),
            jnp.stack(ks_l), jnp.stack(vs_l), jnp.stack(ps_l))
```

```python
import functools
import math

import jax
import jax.numpy as jnp
from jax import lax
from jax.experimental import pallas as pl
from jax.experimental.pallas import tpu as pltpu

F32 = jnp.float32
BF16 = jnp.bfloat16

N_HEADS = 8
HEAD_DIM = 64
ATTN_WIDTH = N_HEADS * HEAD_DIM
MOBA_BLOCK = 256
MOBA_TOPK = 3
POOL_WINDOWS = (2, 4, 8, 16)
POOL_GROUP_WIDTH = 128
POOL_WIDTH = len(POOL_WINDOWS) * POOL_GROUP_WIDTH
POOL_BUF = max(POOL_WINDOWS) - 1
POOL_HALO = 16
ROPE_THETA = 10000.0
RMS_EPS = 1e-6
NEG_INF = -1e30
ATTN_SCALE = HEAD_DIM ** -0.5
LOG2_E = math.log2(math.e)
BF16_SUBLANES = 16
ACC_ROWS = HEAD_DIM + BF16_SUBLANES

LANES = 128
VMEM_LIMIT_BYTES = 60 * 1024 * 1024

PROJ_TILE = 2 * MOBA_BLOCK
KV_UNROLL = 8


def _resident(shape):
    return pl.BlockSpec(shape, lambda *_: (0,) * len(shape), pipeline_mode=pl.Buffered(1))


def _layer_slice(stacked, layer):
    rest = stacked.shape[1:]
    return pl.BlockSpec((None,) + rest, lambda *_: (layer,) + (0,) * len(rest), pipeline_mode=pl.Buffered(1))


def _params(n_axes):
    return pltpu.CompilerParams(dimension_semantics=("arbitrary",) * n_axes,
                                vmem_limit_bytes=VMEM_LIMIT_BYTES)


def _silu(z):
    return z * jax.nn.sigmoid(z)


def _rms_norm_rows(x, w_row):
    ms = jnp.mean(x * x, axis=-1, keepdims=True)
    return x * lax.rsqrt(ms + RMS_EPS) * w_row


def _head_norm_rope(z, ones_bd, cos_t, sin_t, w_rows):
    m = z.shape[0]
    ssq = jnp.dot((z * z).astype(BF16), ones_bd, preferred_element_type=F32)
    r = lax.rsqrt(ssq * (1.0 / HEAD_DIM) + RMS_EPS)
    cw = cos_t * w_rows[0:1, :]
    sw = sin_t * w_rows[1:2, :]
    lane = lax.broadcasted_iota(jnp.int32, (m, LANES), 1)
    first_half = (lane % HEAD_DIM) < (HEAD_DIM // 2)
    outs = []
    for c in range(ATTN_WIDTH // LANES):
        zc = z[:, c * LANES:(c + 1) * LANES]
        partner = jnp.where(first_half,
                            pltpu.roll(zc, LANES - HEAD_DIM // 2, 1),
                            pltpu.roll(zc, HEAD_DIM // 2, 1))
        outs.append((zc * cw + partner * sw) * r[:, c * LANES:(c + 1) * LANES])
    return jnp.concatenate(outs, axis=-1)


def _col_ranges():
    sizes = (ATTN_WIDTH,) * 4 + (POOL_WIDTH,) * 2
    offs = [0]
    for s in sizes:
        offs.append(offs[-1] + s)
    return offs


def _pool_project(d, zgp, zgb, wgrp_ref, pscale_ref, wpb_ref):
    parts = []
    for g in range(len(POOL_WINDOWS)):
        dg = d[:, g * POOL_GROUP_WIDTH:(g + 1) * POOL_GROUP_WIDTH].astype(BF16)
        parts.append(jnp.dot(dg, wgrp_ref[g], preferred_element_type=F32))
    pool = jnp.concatenate(parts, axis=-1) * pscale_ref[...]
    pg = (pool * _silu(zgp)).astype(BF16)
    b = jnp.dot(pg, wpb_ref[...], preferred_element_type=F32)
    return jax.nn.sigmoid(zgb) * b


N_PROJ_PROMPT_OUT = 10


def _proj_prompt_kernel(x_ref, nw_ref, win_ref, ones_ref, cos_ref, sin_ref, qw_ref, kw_ref,
                        wgrp_ref, pscale_ref, wpb_ref, *rest):
    (qt_out, kb_out, kt_out, mean_out, vtf_out, vtb_out,
     sga_out, sig_out, yb_out, plast_out, ubuf) = rest[-(N_PROJ_PROMPT_OUT + 1):]
    earlier = rest[:-(N_PROJ_PROMPT_OUT + 1)]
    own = len(earlier) // 2
    for slot in range(own):
        kt_out[slot] = earlier[2 * slot][0]
        vtf_out[slot] = earlier[2 * slot + 1][0]
    t = pl.program_id(1)
    tm = x_ref.shape[1]
    d_model = x_ref.shape[2]
    offs = _col_ranges()

    @pl.when((pl.program_id(0) == 0) & (t == 0))
    def _():
        ubuf[0:POOL_HALO, :] = jnp.zeros((POOL_HALO, POOL_WIDTH), F32)

    h = _rms_norm_rows(x_ref[0], nw_ref[...]).astype(BF16)

    def proj(lo, hi):
        return jnp.dot(h, win_ref[:, lo:hi], preferred_element_type=F32)

    ones_bd = ones_ref[...]
    cos_t = cos_ref[...]
    sin_t = sin_ref[...]
    n_blk = tm // MOBA_BLOCK
    n_page = tm // LANES

    q = _head_norm_rope(proj(offs[0], offs[1]), ones_bd, cos_t, sin_t, qw_ref[...]) * (ATTN_SCALE * LOG2_E)
    qt = q.T.astype(BF16)
    for i in range(n_blk):
        qt_out[0, i] = qt[:, i * MOBA_BLOCK:(i + 1) * MOBA_BLOCK]

    k = _head_norm_rope(proj(offs[1], offs[2]), ones_bd, cos_t, sin_t, kw_ref[...])
    kb_out[0] = k.astype(BF16)
    kt = k.T
    for i in range(n_page):
        kt_out[own, 0, i] = kt[:, i * LANES:(i + 1) * LANES]
    for i in range(n_blk):
        mean_out[0, i] = jnp.mean(k[i * MOBA_BLOCK:(i + 1) * MOBA_BLOCK], axis=0, keepdims=True)

    vt = proj(offs[2], offs[3]).T
    for i in range(n_page):
        vtf_out[own, 0, i] = vt[:, i * LANES:(i + 1) * LANES]
    vtb = vt.astype(BF16)
    for i in range(n_blk):
        vtb_out[0, i] = vtb[:, i * MOBA_BLOCK:(i + 1) * MOBA_BLOCK]

    sga_out[0] = _silu(proj(offs[3], offs[4])).astype(BF16)

    zu = proj(offs[4], offs[5])
    ubuf[POOL_HALO:POOL_HALO + tm, :] = zu
    pos = t * tm + lax.broadcasted_iota(jnp.int32, (tm, 1), 0)
    parts = []
    for g, w in enumerate(POOL_WINDOWS):
        cols = slice(g * POOL_GROUP_WIDTH, (g + 1) * POOL_GROUP_WIDTH)
        zug = zu[:, cols]
        acc = zug
        for back in range(1, w):
            acc = acc + ubuf[pl.ds(POOL_HALO - back, tm), cols]
        cnt = jnp.minimum(w, pos + 1).astype(F32)
        parts.append(acc / cnt - zug)
    d = jnp.concatenate(parts, axis=-1)
    last = t == pl.num_programs(1) - 1
    ubuf[0:POOL_HALO, :] = jnp.where(last, 0.0, zu[tm - POOL_HALO:tm, :])
    plast_out[0] = zu[tm - POOL_BUF:tm, :]

    zgp = proj(offs[5], offs[6])
    zgb = proj(offs[6] + d_model, offs[6] + 2 * d_model)
    yb_out[0] = _pool_project(d, zgp, zgb, wgrp_ref, pscale_ref, wpb_ref).astype(BF16)
    sig_out[0] = jax.nn.sigmoid(proj(offs[6], offs[6] + d_model)).astype(BF16)


def _proj_prompt(layer, earlier_pages, x, nw, win, ones_bd, cos_t, sin_t, qw, kw, wgrp, pscale, wpb):
    bsz, seq, d_model = x.shape
    tm = PROJ_TILE
    assert seq % tm == 0 and tm % MOBA_BLOCK == 0
    n_t = seq // tm
    n_blk, n_page = seq // MOBA_BLOCK, seq // LANES
    n_slots = len(earlier_pages) // 2 + 1
    tile = lambda width: pl.BlockSpec((1, tm, width), lambda b, t: (b, t, 0))
    paged = lambda per, minor: pl.BlockSpec((1, per, ATTN_WIDTH, minor), lambda b, t: (b, t, 0, 0))
    pages = lambda slots: pl.BlockSpec((slots, 1, tm // LANES, ATTN_WIDTH, LANES), lambda b, t: (0, b, t, 0, 0))
    out_shape = (
        jax.ShapeDtypeStruct((bsz, n_blk, ATTN_WIDTH, MOBA_BLOCK), BF16),
        jax.ShapeDtypeStruct((bsz, seq, ATTN_WIDTH), BF16),
        jax.ShapeDtypeStruct((n_slots, bsz, n_page, ATTN_WIDTH, LANES), F32),
        jax.ShapeDtypeStruct((bsz, n_blk, 1, ATTN_WIDTH), F32),
        jax.ShapeDtypeStruct((n_slots, bsz, n_page, ATTN_WIDTH, LANES), F32),
        jax.ShapeDtypeStruct((bsz, n_blk, ATTN_WIDTH, MOBA_BLOCK), BF16),
        jax.ShapeDtypeStruct((bsz, seq, ATTN_WIDTH), BF16),
        jax.ShapeDtypeStruct((bsz, seq, d_model), BF16),
        jax.ShapeDtypeStruct((bsz, seq, d_model), BF16),
        jax.ShapeDtypeStruct((bsz, POOL_BUF, POOL_WIDTH), F32),
    )
    out_specs = (
        paged(tm // MOBA_BLOCK, MOBA_BLOCK),
        tile(ATTN_WIDTH),
        pages(n_slots),
        pl.BlockSpec((1, tm // MOBA_BLOCK, 1, ATTN_WIDTH), lambda b, t: (b, t, 0, 0)),
        pages(n_slots),
        paged(tm // MOBA_BLOCK, MOBA_BLOCK),
        tile(ATTN_WIDTH),
        tile(d_model),
        tile(d_model),
        pl.BlockSpec((1, POOL_BUF, POOL_WIDTH), lambda b, t: (b, 0, 0)),
    )
    in_specs = [
        tile(d_model),
        _resident(nw.shape), _layer_slice(win, layer), _resident(ones_bd.shape),
        pl.BlockSpec((tm, LANES), lambda b, t: (t, 0)),
        pl.BlockSpec((tm, LANES), lambda b, t: (t, 0)),
        _resident(qw.shape), _resident(kw.shape),
        _layer_slice(wgrp, layer), _resident(pscale.shape), _layer_slice(wpb, layer),
    ]
    in_specs += [pages(1)] * len(earlier_pages)
    args = [x, nw, win, ones_bd, cos_t, sin_t, qw, kw, wgrp, pscale, wpb, *earlier_pages]
    return pl.pallas_call(
        _proj_prompt_kernel,
        grid=(bsz, n_t),
        in_specs=in_specs,
        out_specs=out_specs,
        out_shape=out_shape,
        scratch_shapes=[pltpu.VMEM((POOL_HALO + tm, POOL_WIDTH), F32)],
        compiler_params=_params(2),
        name="proj_prompt",
    )(*args)


def _select_bias(sc, n_valid, n_blk):
    jrow = lax.broadcasted_iota(jnp.int32, sc.shape, 0)
    jrow_f = jrow.astype(F32)
    valid = jrow < n_valid
    s = jnp.where(valid, sc, -jnp.inf)
    for _ in range(MOBA_TOPK):
        top = jnp.max(s, axis=0, keepdims=True)
        first = jnp.min(jnp.where(s == top, jrow_f, float(n_blk)), axis=0, keepdims=True)
        s = jnp.where(jrow_f == first, -jnp.inf, s)
    return jnp.where(valid & (s == -jnp.inf), 0.0, NEG_INF)


def _attn_prompt_tile(i, qt_ref, kb_ref, vtb_ref, mean_ref, qaug_scr, sa_scr, sb_scr, m_scr, acc_scr,
                      before_loops):
    n_blk = mean_ref.shape[1]
    blk = MOBA_BLOCK
    pair_w = 2 * HEAD_DIM
    qt = qt_ref[0, 0]

    means = mean_ref[0].astype(BF16)
    tiled = jnp.concatenate([means] * N_HEADS, axis=0)
    row_head = lax.broadcasted_iota(jnp.int32, tiled.shape, 0) // n_blk
    col_head = lax.broadcasted_iota(jnp.int32, tiled.shape, 1) // HEAD_DIM
    means_bd = jnp.where(row_head == col_head, tiled, jnp.zeros_like(tiled))
    sc = jnp.dot(means_bd, qt, preferred_element_type=F32)

    pair_row = lax.broadcasted_iota(jnp.int32, (pair_w, blk), 0)
    blk_row = lax.broadcasted_iota(jnp.int32, (n_blk, blk), 0)
    tail_row = lax.broadcasted_iota(jnp.int32, (pair_w - n_blk, blk), 0)
    tail = jnp.where(tail_row == 0, NEG_INF, 0.0)
    for h in range(N_HEADS):
        bias = _select_bias(sc[h * n_blk:(h + 1) * n_blk], i, n_blk)
        bias = jnp.where(blk_row == i, 0.0, bias)
        qpair = qt[(h // 2) * pair_w:(h // 2 + 1) * pair_w, :]
        mine = (pair_row < HEAD_DIM) if h % 2 == 0 else (pair_row >= HEAD_DIM)
        qh = jnp.where(mine, qpair, jnp.zeros_like(qpair))
        qaug_scr[h] = jnp.concatenate([qh, jnp.concatenate([bias, tail], axis=0).astype(BF16)], axis=0)

    lane_blk = lax.broadcasted_iota(jnp.int32, (blk, pair_w), 1)

    def keys_aug(j, bias_row):
        start = pl.multiple_of(j * blk, blk)
        onehot = jnp.where(lane_blk == bias_row, 1.0, 0.0).astype(BF16)
        return [jnp.concatenate([kb_ref[0, pl.ds(start, blk), p * pair_w:(p + 1) * pair_w], onehot], axis=1)
                for p in range(N_HEADS // 2)]

    def scores(kj, h):
        return jnp.dot(kj[h // 2], qaug_scr[h], preferred_element_type=F32)

    ones_rows = jnp.ones((ACC_ROWS - HEAD_DIM, blk), BF16)

    def pv_and_sum(h, v_blk, p):
        vt_h = vtb_ref[0, v_blk, h * HEAD_DIM:(h + 1) * HEAD_DIM, :]
        return jnp.dot(jnp.concatenate([vt_h, ones_rows], axis=0), p.astype(BF16), preferred_element_type=F32)

    def first_block(h, st, v_blk):
        m0 = jnp.max(st, axis=0, keepdims=True)
        m_scr[h] = m0
        acc_scr[h] = pv_and_sum(h, v_blk, jnp.exp2(st - m0))

    def next_block(h, sj, v_blk):
        m_old = m_scr[h]
        m_new = jnp.maximum(m_old, jnp.max(sj, axis=0, keepdims=True))
        acc_scr[h] = jnp.exp2(m_old - m_new) * acc_scr[h] + pv_and_sum(h, v_blk, jnp.exp2(sj - m_new))
        m_scr[h] = m_new

    key_i = lax.broadcasted_iota(jnp.int32, (blk, blk), 0)
    qry_i = lax.broadcasted_iota(jnp.int32, (blk, blk), 1)
    causal = key_i <= qry_i
    ka = keys_aug(i, i)
    for h in range(N_HEADS):
        sb_scr[h] = scores(ka, h)
    k0 = keys_aug(0, 0)
    for h in range(N_HEADS):
        sa_scr[h] = scores(k0, h)
        first_block(h, jnp.where(causal, sb_scr[h], NEG_INF), i)

    def make_body(unroll):
        def body(t, base):
            for u in range(unroll):
                cur = base + t * unroll + u
                nxt = cur + 1
                nxt_keys = jnp.minimum(nxt, i - 1)
                k_aug = keys_aug(nxt_keys, jnp.where(nxt < i, nxt, n_blk))
                read, write = (sa_scr, sb_scr) if u % 2 == 0 else (sb_scr, sa_scr)
                for h in range(N_HEADS):
                    write[h] = scores(k_aug, h)
                    next_block(h, read[h], jnp.minimum(cur, i - 1))
            return base
        return body

    before_loops()
    n_wide = i // KV_UNROLL
    lax.fori_loop(0, n_wide, make_body(KV_UNROLL), 0)
    done = n_wide * KV_UNROLL
    lax.fori_loop(0, (i - done + 1) // 2, make_body(2), done)

    outs = []
    for h in range(N_HEADS):
        acc = acc_scr[h]
        outs.append(acc[0:HEAD_DIM] / acc[HEAD_DIM:HEAD_DIM + 1])
    return jnp.concatenate(outs, axis=0).T.astype(BF16)


def _attn_sample_seq(q, k_new, v_new, kt, vt):
    n_new = q.shape[0]
    n_full = kt.shape[1] // MOBA_BLOCK
    tiled = jnp.concatenate([q] * N_HEADS, axis=0)
    row_head = lax.broadcasted_iota(jnp.int32, tiled.shape, 0) // n_new
    col_head = lax.broadcasted_iota(jnp.int32, tiled.shape, 1) // HEAD_DIM
    head_lanes = row_head == col_head
    q_bd = jnp.where(head_lanes, tiled, 0.0).astype(BF16)

    s_past = jnp.dot(q_bd, kt, preferred_element_type=F32)
    nt_dims = (((1,), (1,)), ((), ()))
    s_new = lax.dot_general(q_bd, k_new.astype(BF16), nt_dims, preferred_element_type=F32)
    row_step = lax.broadcasted_iota(jnp.int32, s_new.shape, 0) % n_new
    col_step = lax.broadcasted_iota(jnp.int32, s_new.shape, 1)
    s_new = jnp.where(col_step <= row_step, s_new, NEG_INF)

    blocks = [s_past[:, n * MOBA_BLOCK:(n + 1) * MOBA_BLOCK] for n in range(n_full)]
    score = [jnp.sum(sb, axis=1, keepdims=True) for sb in blocks]
    n_sel = min(MOBA_TOPK, n_full)
    lane = lax.broadcasted_iota(jnp.int32, (tiled.shape[0], LANES), 1)
    by_lane = jnp.full(lane.shape, -jnp.inf, F32)
    for n in range(n_full):
        by_lane = jnp.where(lane == n, score[n], by_lane)
    masked = []
    for n in range(n_full):
        ahead = (by_lane > score[n]) | ((lane < n) & (by_lane == score[n]))
        rank = jnp.sum(jnp.where(ahead, 1.0, 0.0), axis=1, keepdims=True)
        masked.append(blocks[n] + jnp.where(rank < n_sel, 0.0, NEG_INF))

    m = jnp.max(s_new, axis=1, keepdims=True)
    for sb in masked:
        m = jnp.maximum(m, jnp.max(sb, axis=1, keepdims=True))
    p_new = jnp.exp(s_new - m)
    l = jnp.sum(p_new, axis=1, keepdims=True)
    probs = []
    for sb in masked:
        pb = jnp.exp(sb - m)
        l = l + jnp.sum(pb, axis=1, keepdims=True)
        probs.append(pb.astype(BF16))
    p_past = jnp.concatenate(probs, axis=1)
    out = lax.dot_general(p_past, vt, nt_dims, preferred_element_type=F32)
    out = out + jnp.dot(p_new.astype(BF16), v_new.astype(BF16), preferred_element_type=F32)
    out = jnp.where(head_lanes, out / l, 0.0)
    res = out[0:n_new]
    for hh in range(1, N_HEADS):
        res = res + out[hh * n_new:(hh + 1) * n_new]
    return res


def _attn_kernel(layer, pt_ref, qt_ref, kb_ref, vtb_ref, mean_ref, qs_ref, ks_ref, vs_ref, ck_hbm, cv_hbm,
                 x_ref, sga_ref, sig_ref, yb_ref, wpa_ref, wo_ref, o_ref, os_ref, qaug_scr, sa_scr, sb_scr, m_scr, acc_scr, kbuf, vbuf, sem):
    step = pl.program_id(0) * pl.num_programs(1) + pl.program_id(1)
    n_slots = kbuf.shape[0]
    per_step = n_slots - 1
    n_seq = pt_ref.shape[0]
    n_pages = pt_ref.shape[1]
    page = ck_hbm.shape[3]
    n_new = qs_ref.shape[0] // per_step
    first = step * per_step

    def page_copies(seq):
        slot = lax.rem(seq, n_slots)
        copies = []
        for pg in range(n_pages):
            phys = pt_ref[seq, pg]
            win = pl.ds(pg * page, page)
            copies.append(pltpu.make_async_copy(ck_hbm.at[layer, phys], kbuf.at[slot, :, win], sem.at[0, slot]))
            copies.append(pltpu.make_async_copy(cv_hbm.at[layer, phys], vbuf.at[slot, :, win], sem.at[1, slot]))
        return copies

    def start_pages(seq):
        @pl.when(seq < n_seq)
        def _():
            for c in page_copies(seq):
                c.start()

    @pl.when(step == 0)
    def _():
        for j in range(per_step):
            for c in page_copies(j):
                c.start()

    start_pages(first + per_step)
    for j in range(per_step):
        for c in page_copies(first + j):
            c.wait()
    for j in range(per_step):
        rows = slice(j * n_new, (j + 1) * n_new)
        slot = lax.rem(first + j, n_slots)
        os_ref[rows, :] = _attn_sample_seq(qs_ref[rows, :], ks_ref[rows, :], vs_ref[rows, :],
                                           kbuf[slot].astype(BF16), vbuf[slot].astype(BF16))

    def start_rest_of_next_step():
        for j in range(1, per_step):
            start_pages(first + per_step + j)

    attn = _attn_prompt_tile(pl.program_id(1), qt_ref, kb_ref, vtb_ref, mean_ref,
                             qaug_scr, sa_scr, sb_scr, m_scr, acc_scr, start_rest_of_next_step)
    a = jnp.dot(attn * sga_ref[0], wpa_ref[...], preferred_element_type=F32)
    y = sig_ref[0].astype(F32) * a + yb_ref[0].astype(F32)
    o_ref[0] = x_ref[0] + jnp.dot(y.astype(BF16), wo_ref[...], preferred_element_type=F32)


def _attn(layer, page_table, qt, kb, vtb, means, q_s, k_s, v_s, ck_t, cv_t, x, sga, sig, yb, wpa, wo):
    bsz, n_blk, width, blk = qt.shape
    seq = kb.shape[1]
    assert n_blk < 2 * HEAD_DIM
    n_seq, n_pages = page_table.shape
    n_tok = q_s.shape[0]
    n_new = n_tok // n_seq
    page = ck_t.shape[3]
    past = n_pages * page
    assert past % MOBA_BLOCK == 0 and past >= MOBA_BLOCK and ck_t.shape[2] == width
    n_steps = bsz * n_blk
    assert n_seq % n_steps == 0
    per_step = n_seq // n_steps
    sample_tile = pl.BlockSpec((per_step * n_new, width), lambda b, i, pt: (b * n_blk + i, 0))
    d_model = x.shape[2]
    q_tile = lambda w: pl.BlockSpec((1, blk, w), lambda b, i, pt: (b, i, 0))
    grid_spec = pltpu.PrefetchScalarGridSpec(
        num_scalar_prefetch=1,
        grid=(bsz, n_blk),
        in_specs=[
            pl.BlockSpec((1, 1, width, blk), lambda b, i, pt: (b, i, 0, 0)),
            pl.BlockSpec((1, seq, width), lambda b, i, pt: (b, 0, 0), pipeline_mode=pl.Buffered(1)),
            pl.BlockSpec((1, n_blk, width, blk), lambda b, i, pt: (b, 0, 0, 0), pipeline_mode=pl.Buffered(1)),
            pl.BlockSpec((1, n_blk, width), lambda b, i, pt: (b, 0, 0)),
            sample_tile, sample_tile, sample_tile,
            pl.BlockSpec(memory_space=pl.ANY), pl.BlockSpec(memory_space=pl.ANY),
            q_tile(d_model), q_tile(width), q_tile(d_model), q_tile(d_model),
            _layer_slice(wpa, layer), _layer_slice(wo, layer),
        ],
        out_specs=(q_tile(d_model), sample_tile),
        scratch_shapes=[pltpu.VMEM((N_HEADS, 4 * HEAD_DIM, blk), BF16),
                        pltpu.VMEM((N_HEADS, blk, blk), F32),
                        pltpu.VMEM((N_HEADS, blk, blk), F32),
                        pltpu.VMEM((N_HEADS, 1, blk), F32),
                        pltpu.VMEM((N_HEADS, ACC_ROWS, blk), F32),
                        pltpu.VMEM((per_step + 1, width, past), F32),
                        pltpu.VMEM((per_step + 1, width, past), F32),
                        pltpu.SemaphoreType.DMA((2, per_step + 1))],
    )
    return pl.pallas_call(
        functools.partial(_attn_kernel, layer),
        grid_spec=grid_spec,
        out_shape=(jax.ShapeDtypeStruct((bsz, seq, d_model), F32), jax.ShapeDtypeStruct((n_tok, width), F32)),
        compiler_params=_params(2),
        name="attn",
    )(page_table, qt, kb, vtb, means, q_s, k_s, v_s, ck_t, cv_t, x, sga, sig, yb, wpa, wo)


def _merge_kernel(x_ref, attn_ref, sga_ref, sig_ref, yb_ref, wpa_ref, wo_ref, o_ref):
    gated = attn_ref[...].astype(BF16) * sga_ref[...]
    a = jnp.dot(gated, wpa_ref[...], preferred_element_type=F32)
    y = sig_ref[...].astype(F32) * a + yb_ref[...].astype(F32)
    o_ref[...] = x_ref[...] + jnp.dot(y.astype(BF16), wo_ref[...], preferred_element_type=F32)


def _merge(layer, x, attn, sga, sig, yb, wpa, wo, tm):
    n, d_model = x.shape
    assert n % tm == 0
    tile = lambda width: pl.BlockSpec((tm, width), lambda t: (t, 0))
    return pl.pallas_call(
        _merge_kernel,
        grid=(n // tm,),
        in_specs=[tile(d_model), tile(ATTN_WIDTH), tile(ATTN_WIDTH), tile(d_model), tile(d_model),
                  _layer_slice(wpa, layer), _layer_slice(wo, layer)],
        out_specs=tile(d_model),
        out_shape=jax.ShapeDtypeStruct((n, d_model), F32),
        compiler_params=_params(1),
        name="merge",
    )(x, attn, sga, sig, yb, wpa, wo)


def _proj_sample_kernel(past_len, x_ref, nw_ref, win_ref, ones_ref, cos_ref, sin_ref, qw_ref, kw_ref,
                        wgrp_ref, pscale_ref, wpb_ref, state_ref,
                        q_out, k_out, v_out, kst_out, vst_out, sga_out, sig_out, yb_out, pool_out,
                        slab_scr):
    n_tok, d_model = x_ref.shape
    n_seq = state_ref.shape[1]
    n_new = n_tok // n_seq
    n_slab = slab_scr.shape[0]
    offs = _col_ranges()
    h = _rms_norm_rows(x_ref[...], nw_ref[...]).astype(BF16)

    def proj(lo, hi):
        return jnp.dot(h, win_ref[:, lo:hi], preferred_element_type=F32)

    def to_slabs(val):
        for c in range(n_slab):
            slab_scr[c] = val[:, c * LANES:(c + 1) * LANES]

    def step_rows(s):
        return jnp.concatenate(
            [slab_scr[c, pl.ds(s, n_seq, stride=n_new), :] for c in range(n_slab)], axis=-1)

    ones_bd = ones_ref[...]
    cos_t = cos_ref[...]
    sin_t = sin_ref[...]
    q_out[...] = _head_norm_rope(proj(offs[0], offs[1]), ones_bd, cos_t, sin_t, qw_ref[...]) * ATTN_SCALE

    k = _head_norm_rope(proj(offs[1], offs[2]), ones_bd, cos_t, sin_t, kw_ref[...])
    k_out[...] = k
    to_slabs(k)
    for s in range(n_new):
        kst_out[s] = step_rows(s).T

    v = proj(offs[2], offs[3])
    v_out[...] = v
    to_slabs(v)
    for s in range(n_new):
        vst_out[s] = step_rows(s).T

    sga_out[...] = _silu(proj(offs[3], offs[4])).astype(BF16)

    zu = proj(offs[4], offs[5])
    to_slabs(zu)
    hist = [state_ref[j] for j in range(POOL_BUF)] + [step_rows(s) for s in range(n_new)]
    for j in range(POOL_BUF):
        pool_out[j] = hist[len(hist) - POOL_BUF + j]
    for s in range(n_new):
        cur = POOL_BUF + s
        parts = []
        for g, w in enumerate(POOL_WINDOWS):
            cols = slice(g * POOL_GROUP_WIDTH, (g + 1) * POOL_GROUP_WIDTH)
            acc = hist[cur][:, cols]
            for back in range(1, w):
                acc = acc + hist[cur - back][:, cols]
            cnt = float(min(w, past_len + s + 1))
            parts.append(acc / cnt - hist[cur][:, cols])
        ds = jnp.concatenate(parts, axis=-1)
        for c in range(n_slab):
            slab_scr[c, pl.ds(s, n_seq, stride=n_new), :] = ds[:, c * LANES:(c + 1) * LANES]
    d = jnp.concatenate([slab_scr[c] for c in range(n_slab)], axis=-1)

    zgp = proj(offs[5], offs[6])
    zgb = proj(offs[6] + d_model, offs[6] + 2 * d_model)
    yb_out[...] = _pool_project(d, zgp, zgb, wgrp_ref, pscale_ref, wpb_ref).astype(BF16)
    sig_out[...] = jax.nn.sigmoid(proj(offs[6], offs[6] + d_model)).astype(BF16)


def _proj_sample(layer, x, nw, win, ones_bd, cos_t, sin_t, qw, kw, wgrp, pscale, wpb, state_t, past_len):
    n_tok, d_model = x.shape
    _, n_hist, n_seq, pool_w = state_t.shape
    assert n_hist == POOL_BUF and pool_w == POOL_WIDTH and n_tok % n_seq == 0
    n_new = n_tok // n_seq
    assert ATTN_WIDTH == POOL_WIDTH
    n_slab = POOL_WIDTH // LANES
    full = lambda shape: pl.BlockSpec(shape, lambda t: (0,) * len(shape))
    out_shape = (
        jax.ShapeDtypeStruct((n_tok, ATTN_WIDTH), F32),
        jax.ShapeDtypeStruct((n_tok, ATTN_WIDTH), F32),
        jax.ShapeDtypeStruct((n_tok, ATTN_WIDTH), F32),
        jax.ShapeDtypeStruct((n_new, ATTN_WIDTH, n_seq), F32),
        jax.ShapeDtypeStruct((n_new, ATTN_WIDTH, n_seq), F32),
        jax.ShapeDtypeStruct((n_tok, ATTN_WIDTH), BF16),
        jax.ShapeDtypeStruct((n_tok, d_model), BF16),
        jax.ShapeDtypeStruct((n_tok, d_model), BF16),
        jax.ShapeDtypeStruct((POOL_BUF, n_seq, POOL_WIDTH), F32),
    )
    args = (x, nw, win, ones_bd, cos_t, sin_t, qw, kw, wgrp, pscale, wpb, state_t)
    return pl.pallas_call(
        functools.partial(_proj_sample_kernel, past_len),
        grid=(1,),
        in_specs=[_layer_slice(a, layer) if any(a is w for w in (win, wgrp, wpb, state_t)) else full(a.shape)
                  for a in args],
        out_specs=tuple(full(o.shape) for o in out_shape),
        out_shape=out_shape,
        scratch_shapes=[pltpu.VMEM((n_slab, n_tok, LANES), F32)],
        compiler_params=_params(1),
        name="proj_sample",
    )(*args)


def _rope_tables(pos):
    half = HEAD_DIM // 2
    inv_freq = jnp.exp(-math.log(ROPE_THETA) * jnp.arange(half, dtype=F32) / half)
    ang = pos.astype(F32)[:, None] * inv_freq[None, :]
    cos, sin = jnp.cos(ang), jnp.sin(ang)
    reps = LANES // HEAD_DIM
    return (jnp.tile(jnp.concatenate([cos, cos], axis=-1), (1, reps)),
            jnp.tile(jnp.concatenate([-sin, sin], axis=-1), (1, reps)))


def _norm_rows(w):
    half = HEAD_DIM // 2
    reps = LANES // HEAD_DIM
    swapped = jnp.concatenate([w[half:], w[:half]])
    return jnp.stack([jnp.tile(w, reps), jnp.tile(swapped, reps)]).astype(F32)


def kernel(x_prompt, x_sample, cache_k, cache_v, state_pool, page_table, norm_w, w_in, q_norm_w,
           k_norm_w, w_pool_grp, pool_scale, w_proj_attn, w_proj_pool, w_out):
    bp, sp, d_model = x_prompt.shape
    bs, ss, _ = x_sample.shape
    depth, n_phys, page, n_heads, head_dim = cache_k.shape
    assert (n_heads, head_dim) == (N_HEADS, HEAD_DIM) and page == LANES
    n_pages = page_table.shape[1]
    past_len = n_pages * page

    cos_p, sin_p = _rope_tables(jnp.arange(sp, dtype=jnp.int32))
    cos_s, sin_s = _rope_tables(past_len + jnp.arange(ss, dtype=jnp.int32))
    cos_s, sin_s = jnp.tile(cos_s, (bs, 1)), jnp.tile(sin_s, (bs, 1))
    ones_bd = jnp.kron(jnp.eye(N_HEADS, dtype=F32), jnp.ones((HEAD_DIM, HEAD_DIM), F32)).astype(BF16)

    ck_t = cache_k.transpose(0, 1, 3, 4, 2).reshape(depth, n_phys, ATTN_WIDTH, page)
    cv_t = cache_v.transpose(0, 1, 3, 4, 2).reshape(depth, n_phys, ATTN_WIDTH, page)
    state_t = state_pool.transpose(0, 2, 1, 3)

    w_in_b = w_in.astype(BF16)
    w_grp_b = w_pool_grp.astype(BF16)
    w_pa_b = w_proj_attn.astype(BF16)
    w_pb_b = w_proj_pool.astype(BF16)
    w_o_b = w_out.astype(BF16)

    xp = x_prompt
    xs = x_sample.reshape(bs * ss, d_model)
    kv_pages = []
    pp_l, ks_l, vs_l, ps_l = [], [], [], []
    for l in range(depth):
        nw = norm_w[l][None, :]
        qw, kw = _norm_rows(q_norm_w[l]), _norm_rows(k_norm_w[l])
        pscale = pool_scale[l][None, :]

        qt, kb, kt, means, vtf, vtb, sga, sig, yb, plast = _proj_prompt(
            l, kv_pages if l == depth - 1 else [], xp, nw, w_in_b, ones_bd, cos_p, sin_p, qw, kw,
            w_grp_b, pscale, w_pb_b)
        kv_pages += [kt, vtf]
        pp_l.append(plast)
        q_s, k_s, v_s, kst, vst, sga_s, sig_s, yb_s, pool_s = _proj_sample(
            l, xs, nw, w_in_b, ones_bd, cos_s, sin_s, qw, kw, w_grp_b, pscale, w_pb_b, state_t, past_len)

        xp, attn_s = _attn(l, page_table, qt, kb, vtb, means.reshape(bp, -1, ATTN_WIDTH),
                           q_s, k_s, v_s, ck_t, cv_t, xp, sga, sig, yb, w_pa_b, w_o_b)
        xs = _merge(l, xs, attn_s, sga_s, sig_s, yb_s, w_pa_b, w_o_b, bs * ss)
        to_steps = lambda a: a.reshape(ss, N_HEADS, HEAD_DIM, bs).transpose(3, 0, 1, 2)
        ks_l.append(to_steps(kst))
        vs_l.append(to_steps(vst))
        ps_l.append(pool_s.transpose(1, 0, 2))

    to_pages = lambda a: a.reshape(depth, bp, sp // page, N_HEADS, HEAD_DIM, page).transpose(0, 1, 2, 5, 3, 4)
    return (xp, xs.reshape(bs, ss, d_model), to_pages(kv_pages[-2]), to_pages(kv_pages[-1]), jnp.stack(pp_l),
            jnp.stack(ks_l), jnp.stack(vs_l), jnp.stack(ps_l))
```
